```python
import math
import jax, jax.numpy as jnp
from jax import lax
import numpy as np

D_MODEL = 2048
BATCH = 16
SEQ = 2048
DEPTH = 4

F32 = jnp.float32
N_MEM = 256
Q_BLOCK = 128
DIFF_HEADS = D_MODEL // 256
DIFF_DQK = 32
DIFF_DV = 2 * DIFF_DQK
SWA_HEADS = D_MODEL // 128
SWA_KV_HEADS = 2
SWA_DH = 64
SWA_WINDOW = 128
GLA_HEADS = D_MODEL // 512
GLA_DK = 64
GLA_DV = 128
GLA_RANK = 16
GLA_TAU = 16.0
GLA_CHUNK = 64
D_DIFF = DIFF_HEADS * DIFF_DV
D_SWA = SWA_HEADS * SWA_DH
D_GLA = GLA_HEADS * GLA_DV
D_MIX = D_DIFF + D_SWA + D_GLA
IN_WIDTHS = (
    DIFF_HEADS * 2 * DIFF_DQK,
    DIFF_HEADS * 2 * DIFF_DQK,
    D_DIFF,
    D_SWA,
    SWA_KV_HEADS * SWA_DH,
    SWA_KV_HEADS * SWA_DH,
    GLA_HEADS * GLA_DK,
    GLA_HEADS * GLA_DK,
    D_GLA,
    GLA_RANK,
    D_GLA,
)
D_IN = sum(IN_WIDTHS)
SPLIT_POINTS = [int(v) for v in np.cumsum(IN_WIDTHS)[:-1]]
XA_HEADS = 4
XA_DH = 128
D_XA = XA_HEADS * XA_DH
D_FF = int(math.ceil(8 * D_MODEL / 3 / 256)) * 256
CONV_W = 3
EPS = 1e-6

kernel_name = "hybrid_diff_swa_gla_trunk"


def rmsnorm(x, g):
    xf = x.astype(F32)
    y = xf * lax.rsqrt(jnp.mean(xf * xf, axis=-1, keepdims=True) + EPS)
    return (y * g.astype(F32)).astype(x.dtype)


def alibi_slopes(n):
    return 2.0 ** (-8.0 * jnp.arange(1, n + 1, dtype=F32) / n)


def diff_attention(q, k, v, lam, subln_g, lambda_init):
    B, S, H, _, dq = q.shape
    dv = v.shape[-1]
    nb = S // Q_BLOCK
    scale = dq ** -0.5
    slopes = alibi_slopes(H)
    lamf = lam.astype(F32)
    lam_full = jnp.exp(jnp.sum(lamf[0] * lamf[1])) - jnp.exp(jnp.sum(lamf[2] * lamf[3])) + lambda_init
    pos_k = jnp.arange(S)
    qb = q.reshape(B, nb, Q_BLOCK, H, 2, dq).transpose(1, 0, 2, 3, 4, 5)

    def block(args):
        qi, n = args
        pos_q = n * Q_BLOCK + jnp.arange(Q_BLOCK)
        dist = pos_q[:, None] - pos_k[None, :]
        s = jnp.einsum('bqhcd,bkhcd->bhcqk', qi, k).astype(F32) * scale
        s = s - slopes[None, :, None, None, None] * dist.astype(F32)
        s = jnp.where(dist >= 0, s, -jnp.inf)
        p = jax.nn.softmax(s, axis=-1)
        a = p[:, :, 0] - lam_full * p[:, :, 1]
        return jnp.einsum('bhqk,bkhd->bqhd', a.astype(v.dtype), v)

    o = lax.map(block, (qb, jnp.arange(nb)))
    o = o.transpose(1, 0, 2, 3, 4).reshape(B, S, H, dv)
    o = rmsnorm(o, subln_g) * (1.0 - lambda_init)
    return o.reshape(B, S, H * dv)


def swa_attention(q, k, v, sinks):
    B, S, Hq, dh = q.shape
    G = k.shape[2]
    R = Hq // G
    W = SWA_WINDOW
    nb = S // W
    slopes = alibi_slopes(Hq).reshape(G, R)
    qb = q.reshape(B, nb, W, G, R, dh)

    def with_prev(t):
        tb = t.reshape(B, nb, W, G, dh)
        prev = jnp.concatenate([jnp.zeros_like(tb[:, :1]), tb[:, :-1]], axis=1)
        return jnp.concatenate([prev, tb], axis=2)

    kw, vw = with_prev(k), with_prev(v)
    s = jnp.einsum('bnqgrd,bnkgd->bngrqk', qb, kw).astype(F32) * (dh ** -0.5)
    i = jnp.arange(W)[:, None]
    j = jnp.arange(2 * W)[None, :]
    dist = i + W - j
    blk = jnp.arange(nb)[:, None, None]
    valid = (dist >= 0) & (dist < W) & (blk * W - W + j[None] >= 0)
    s = s - slopes[:, :, None, None] * dist.astype(F32)
    s = jnp.where(valid[None, :, None, None], s, -jnp.inf)
    sink = jnp.broadcast_to(sinks.astype(F32).reshape(G, R)[None, None, :, :, None, None], s.shape[:-1] + (1,))
    p = jax.nn.softmax(jnp.concatenate([s, sink], axis=-1), axis=-1)[..., :-1]
    o = jnp.einsum('bngrqk,bnkgd->bnqgrd', p.astype(v.dtype), vw)
    return o.reshape(B, S, Hq * dh)


def gla_attention(q, k, v, log_a):
    B, S, H, dk = q.shape
    dv = v.shape[-1]
    C = GLA_CHUNK
    nc = S // C

    def chunk(t):
        return t.astype(F32).reshape(B, nc, C, H, t.shape[-1])

    qc = chunk(q) * (dk ** -0.5)
    kc, vc, gc = chunk(k), chunk(v), chunk(log_a)
    b = jnp.cumsum(gc, axis=2)
    b_last = b[:, :, -1:]
    q_dec = qc * jnp.exp(b)
    k_inv = kc * jnp.exp(-b)
    k_end = kc * jnp.exp(b_last - b)
    causal = jnp.tril(jnp.ones((C, C), dtype=bool))
    attn = jnp.where(causal, jnp.einsum('bnthd,bnshd->bnhts', q_dec, k_inv), 0.0)
    o_intra = jnp.einsum('bnhts,bnshv->bnthv', attn, vc)
    kv = jnp.einsum('bnshd,bnshv->bnhdv', k_end, vc)
    decay = jnp.exp(b_last[:, :, 0])

    def step(state, inp):
        kv_n, dec_n = inp
        return state * dec_n[..., None] + kv_n, state

    init = jnp.zeros((B, H, dk, dv), F32)
    _, states = lax.scan(step, init, (kv.transpose(1, 0, 2, 3, 4), decay.transpose(1, 0, 2, 3)))
    states = states.transpose(1, 0, 2, 3, 4)
    o_inter = jnp.einsum('bnthd,bnhdv->bnthv', q_dec, states)
    return (o_intra + o_inter).reshape(B, S, H, dv)


def cross_attention(xn, memn, wq, wkv, wo):
    B, S, _ = xn.shape
    M = memn.shape[1]
    q = (xn @ wq).reshape(B, S, XA_HEADS, XA_DH)
    k, v = jnp.split(memn @ wkv, 2, axis=-1)
    k = k.reshape(B, M, XA_HEADS, XA_DH)
    v = v.reshape(B, M, XA_HEADS, XA_DH)
    s = jnp.einsum('bshd,bmhd->bhsm', q, k).astype(F32) * (XA_DH ** -0.5)
    p = jax.nn.softmax(s, axis=-1)
    o = jnp.einsum('bhsm,bmhd->bshd', p.astype(v.dtype), v).reshape(B, S, D_XA)
    return o @ wo


def conv_glu(xn, w_up, conv_w, conv_b, w_down):
    h = xn @ w_up
    hp = jnp.pad(h, ((0, 0), (CONV_W - 1, 0), (0, 0)))
    h = hp[:, :-2] * conv_w[0] + hp[:, 1:-1] * conv_w[1] + hp[:, 2:] * conv_w[2] + conv_b
    gate, up = jnp.split(h, 2, axis=-1)
    return (jax.nn.silu(gate) * up) @ w_down


def setup_inputs(seed: int = 0) -> dict:
    key = jax.random.key(seed)
    ks = jax.random.split(key, 24)
    L, D = DEPTH, D_MODEL

    def nrm(k, shape, scale):
        return jax.random.normal(k, shape, F32) * scale

    def gain(k, shape):
        return 1.0 + 0.02 * jax.random.normal(k, shape, F32)

    return {
        "x": nrm(ks[0], (BATCH, SEQ, D), 1.0),
        "mem": nrm(ks[1], (BATCH, N_MEM, D), 1.0),
        "norm_mix_g": gain(ks[2], (L, D)),
        "w_in": nrm(ks[3], (L, D, D_IN), D ** -0.5),
        "diff_lambda": nrm(ks[4], (L, 4, DIFF_DQK), 0.1),
        "diff_subln_g": gain(ks[5], (L, DIFF_DV)),
        "swa_sinks": nrm(ks[6], (L, SWA_HEADS), 1.0),
        "gla_gate_w2": nrm(ks[7], (L, GLA_RANK, GLA_HEADS * GLA_DK), GLA_RANK ** -0.5),
        "gla_gate_b": nrm(ks[8], (L, GLA_HEADS * GLA_DK), 0.02),
        "gla_norm_g": gain(ks[9], (L, GLA_DV)),
        "w_out": nrm(ks[10], (L, D_MIX, D), D_MIX ** -0.5),
        "norm_xa_g": gain(ks[11], (L, D)),
        "norm_mem_g": gain(ks[12], (L, D)),
        "xa_wq": nrm(ks[13], (L, D, D_XA), D ** -0.5),
        "xa_wkv": nrm(ks[14], (L, D, 2 * D_XA), D ** -0.5),
        "xa_wo": nrm(ks[15], (L, D_XA, D), D_XA ** -0.5),
        "norm_ffn_g": gain(ks[16], (L, D)),
        "ffn_w_up": nrm(ks[17], (L, D, 2 * D_FF), D ** -0.5),
        "ffn_conv_w": nrm(ks[18], (L, CONV_W, 2 * D_FF), CONV_W ** -0.5),
        "ffn_conv_b": nrm(ks[19], (L, 2 * D_FF), 0.02),
        "ffn_w_down": nrm(ks[20], (L, D_FF, D), D_FF ** -0.5),
        "final_norm_g": gain(ks[21], (D,)),
    }


def reference(x, mem, norm_mix_g, w_in, diff_lambda, diff_subln_g, swa_sinks, gla_gate_w2, gla_gate_b, gla_norm_g, w_out, norm_xa_g, norm_mem_g, xa_wq, xa_wkv, xa_wo, norm_ffn_g, ffn_w_up, ffn_conv_w, ffn_conv_b, ffn_w_down, final_norm_g):
    B, S, _ = x.shape
    h = x
    for l in range(DEPTH):
        xn = rmsnorm(h, norm_mix_g[l])
        (d_q, d_k, d_v, s_q, s_k, s_v, g_q, g_k, g_v, g_lr, g_og) = jnp.split(xn @ w_in[l], SPLIT_POINTS, axis=-1)
        lambda_init = 0.8 - 0.6 * math.exp(-0.3 * l)
        y_diff = diff_attention(
            d_q.reshape(B, S, DIFF_HEADS, 2, DIFF_DQK),
            d_k.reshape(B, S, DIFF_HEADS, 2, DIFF_DQK),
            d_v.reshape(B, S, DIFF_HEADS, DIFF_DV),
            diff_lambda[l], diff_subln_g[l], lambda_init)
        y_swa = swa_attention(
            s_q.reshape(B, S, SWA_HEADS, SWA_DH),
            s_k.reshape(B, S, SWA_KV_HEADS, SWA_DH),
            s_v.reshape(B, S, SWA_KV_HEADS, SWA_DH),
            swa_sinks[l])
        z = (g_lr @ gla_gate_w2[l] + gla_gate_b[l]).astype(F32)
        log_a = (jax.nn.log_sigmoid(z) / GLA_TAU).reshape(B, S, GLA_HEADS, GLA_DK)
        o_gla = gla_attention(
            g_q.reshape(B, S, GLA_HEADS, GLA_DK),
            g_k.reshape(B, S, GLA_HEADS, GLA_DK),
            g_v.reshape(B, S, GLA_HEADS, GLA_DV),
            log_a)
        y_gla = (rmsnorm(o_gla, gla_norm_g[l]).reshape(B, S, D_GLA) * jax.nn.silu(g_og.astype(F32))).astype(x.dtype)
        mix = jnp.concatenate([y_diff.astype(x.dtype), y_swa.astype(x.dtype), y_gla], axis=-1)
        h = h + mix @ w_out[l]
        h = h + cross_attention(rmsnorm(h, norm_xa_g[l]), rmsnorm(mem, norm_mem_g[l]), xa_wq[l], xa_wkv[l], xa_wo[l])
        h = h + conv_glu(rmsnorm(h, norm_ffn_g[l]), ffn_w_up[l], ffn_conv_w[l], ffn_conv_b[l], ffn_w_down[l])
    return rmsnorm(h, final_norm_g)
```

```python
import functools
import math

import jax
import jax.numpy as jnp
import numpy as np
from jax import lax
from jax.experimental import pallas as pl
from jax.experimental.pallas import tpu as pltpu

F32 = jnp.float32
BF16 = jnp.bfloat16
EPS = 1e-6
NEG = -1e30

LANES = 128
VMEM_LIMIT = 56 * 1024 * 1024

DIFF_HEADS = 8
DIFF_DQK = 32
DIFF_DV = 64
SWA_HEADS = 16
SWA_KV_HEADS = 2
SWA_DH = 64
SWA_WINDOW = 128
GLA_HEADS = 4
GLA_DK = 64
GLA_DV = 128
GLA_RANK = 16
GLA_TAU = 16.0
GLA_CHUNK = 64
XA_HEADS = 4
XA_DH = 128
CONV_W = 3
CONV_HALO = 16

C_DQ, C_DK, C_DV = 0, 512, 1024
C_SQ, C_SK, C_SV = 1536, 2560, 2688
C_GQ, C_GK, C_GV = 2816, 3072, 3328
C_OG, C_LR = 3840, 4352
N_PROJ = 4608


def _cparams(sem):
    return pltpu.CompilerParams(dimension_semantics=sem, vmem_limit_bytes=VMEM_LIMIT)


def _norm_matmul_kernel(x_ref, g_ref, w_ref, o_ref, xn_ref):
    @pl.when(pl.program_id(1) == 0)
    def _():
        x = x_ref[...]
        ms = jnp.mean(x * x, axis=-1, keepdims=True)
        xn_ref[...] = (x * lax.rsqrt(ms + EPS) * g_ref[...]).astype(BF16)

    o_ref[...] = jnp.dot(xn_ref[...], w_ref[...], preferred_element_type=F32).astype(o_ref.dtype)


def norm_matmul(x, g, w, *, tm, tn, out_dtype=BF16):
    m, k = x.shape
    n = w.shape[1]
    return pl.pallas_call(
        _norm_matmul_kernel,
        out_shape=jax.ShapeDtypeStruct((m, n), out_dtype),
        grid=(m // tm, n // tn),
        in_specs=[
            pl.BlockSpec((tm, k), lambda i, j: (i, 0)),
            pl.BlockSpec((1, k), lambda i, j: (0, 0)),
            pl.BlockSpec((k, tn), lambda i, j: (0, j)),
        ],
        out_specs=pl.BlockSpec((tm, tn), lambda i, j: (i, j)),
        scratch_shapes=[pltpu.VMEM((tm, k), BF16)],
        compiler_params=_cparams(("parallel", "arbitrary")),
        name="norm_matmul",
    )(x, g.reshape(1, k), w)


def _diff_kernel(lam_ref, linit_ref, g_ref, q_ref, k_ref, v_ref, o_ref, m_ref, l_ref, acc_ref, *, tq, tk):
    hp = pl.program_id(1)
    qi = pl.program_id(2)
    rows = 4 * tq

    q = q_ref[...]
    lane = lax.broadcasted_iota(jnp.int32, (tq, LANES), 1)
    zero = jnp.zeros_like(q)
    q4 = jnp.concatenate([jnp.where((lane // DIFF_DQK) == c, q, zero) for c in range(4)], axis=0)

    row = lax.broadcasted_iota(jnp.int32, (rows, 1), 0)
    head = (2 * hp + row // (2 * tq)).astype(F32)
    slope = jnp.exp2(-8.0 * (head + 1.0) / DIFF_HEADS)
    ii = lax.broadcasted_iota(jnp.int32, (rows, tk), 0) % tq
    jj = lax.broadcasted_iota(jnp.int32, (rows, tk), 1)
    base_d = ii - jj
    bias1 = slope * base_d.astype(F32)

    m_ref[...] = jnp.full_like(m_ref, NEG)
    l_ref[...] = jnp.zeros_like(l_ref)
    acc_ref[...] = jnp.zeros_like(acc_ref)

    def kblock(kj, masked):
        off = qi * tq - kj * tk
        start = pl.multiple_of(kj * tk, tk)
        kb = k_ref[pl.ds(start, tk), :]
        vb = v_ref[pl.ds(start, tk), :]
        s = lax.dot_general(q4, kb, (((1,), (1,)), ((), ())), preferred_element_type=F32)
        u = s - bias1
        if masked:
            u = jnp.where(base_d + off >= 0, u, NEG)
        cvec = slope * off.astype(F32)
        m_old = m_ref[...]
        m_new = jnp.maximum(m_old, jnp.max(u, axis=-1, keepdims=True) - cvec)
        p = jnp.exp(u - (m_new + cvec))
        alpha = jnp.exp(m_old - m_new)
        l_ref[...] = alpha * l_ref[...] + jnp.sum(p, axis=-1, keepdims=True)
        acc_ref[...] = alpha * acc_ref[...] + jnp.dot(p.astype(BF16), vb, preferred_element_type=F32)
        m_ref[...] = m_new

    nfull = (qi * tq) // tk

    def body(kj, carry):
        kblock(kj, False)
        return carry

    lax.fori_loop(0, nfull, body, 0)
    kblock(nfull, True)

    lamv = lam_ref[...]
    lam1 = jnp.exp(jnp.sum(lamv[0:1] * lamv[1:2], axis=-1, keepdims=True))
    lam2 = jnp.exp(jnp.sum(lamv[2:3] * lamv[3:4], axis=-1, keepdims=True))
    linit = linit_ref[...]
    lam_full = lam1 - lam2 + linit

    accn = acc_ref[...] / l_ref[...]
    a0 = accn[0:tq] - lam_full * accn[tq:2 * tq]
    a1 = accn[2 * tq:3 * tq] - lam_full * accn[3 * tq:4 * tq]
    left = lane < DIFF_DV
    o = jnp.where(left, a0, a1)
    sq = o * o
    ms0 = jnp.sum(jnp.where(left, sq, 0.0), axis=-1, keepdims=True) / DIFF_DV
    ms1 = jnp.sum(jnp.where(left, 0.0, sq), axis=-1, keepdims=True) / DIFF_DV
    ms = jnp.where(left, ms0, ms1)
    y = o * lax.rsqrt(ms + EPS) * g_ref[...] * (1.0 - linit)
    o_ref[...] = y.astype(o_ref.dtype)


def diff_attention(proj, lam, linit, subln_g, *, batch, seq, tq=128, tk=256):
    t = batch * seq
    nq = seq // tq
    g2 = jnp.concatenate([subln_g, subln_g]).reshape(1, LANES).astype(F32)
    kern = functools.partial(_diff_kernel, tq=tq, tk=tk)
    return pl.pallas_call(
        kern,
        out_shape=jax.ShapeDtypeStruct((t, DIFF_HEADS * DIFF_DV), BF16),
        grid=(batch, DIFF_HEADS // 2, nq),
        in_specs=[
            pl.BlockSpec((4, DIFF_DQK), lambda b, h, i: (0, 0)),
            pl.BlockSpec((1, 1), lambda b, h, i: (0, 0)),
            pl.BlockSpec((1, LANES), lambda b, h, i: (0, 0)),
            pl.BlockSpec((tq, LANES), lambda b, h, i: (b * nq + i, C_DQ // LANES + h)),
            pl.BlockSpec((seq, LANES), lambda b, h, i: (b, C_DK // LANES + h)),
            pl.BlockSpec((seq, LANES), lambda b, h, i: (b, C_DV // LANES + h)),
        ],
        out_specs=pl.BlockSpec((tq, LANES), lambda b, h, i: (b * nq + i, h)),
        scratch_shapes=[
            pltpu.VMEM((4 * tq, 1), F32),
            pltpu.VMEM((4 * tq, 1), F32),
            pltpu.VMEM((4 * tq, LANES), F32),
        ],
        compiler_params=_cparams(("parallel", "parallel", "arbitrary")),
        name="diff_attention",
    )(lam.astype(F32), linit, g2, proj, proj, proj)


def _swa_dist_tables():
    w = SWA_WINDOW
    i = np.arange(w)[:, None]
    j = np.arange(2 * w)[None, :]
    dist = (i + w - j).astype(np.float32)
    valid = (dist >= 0) & (dist < w)
    general = np.where(valid, dist, 1e30)
    first = np.where(valid & (j >= w), dist, 1e30)
    return jnp.asarray(np.stack([first, general]).astype(np.float32))


def _swa_kernel(c_ref, d_ref, q_ref, kp_ref, kc_ref, vp_ref, vc_ref, o_ref):
    g = pl.program_id(1)
    w = SWA_WINDOW
    rper = SWA_HEADS // SWA_KV_HEADS
    lane = lax.broadcasted_iota(jnp.int32, (2 * w, LANES), 1)
    mine = (lane // SWA_DH) == g

    def dup(prev_ref, cur_ref):
        x = jnp.concatenate([prev_ref[...], cur_ref[...]], axis=0).astype(F32)
        return jnp.where(mine, x, pltpu.roll(x, SWA_DH, axis=1)).astype(BF16)

    kk = dup(kp_ref, kc_ref)
    vv = dup(vp_ref, vc_ref)
    dist = d_ref[0]
    qlane = lax.broadcasted_iota(jnp.int32, (w, LANES), 1)
    left = qlane < SWA_DH

    for p in range(rper // 2):
        q2 = q_ref[:, p * LANES:(p + 1) * LANES]
        zero = jnp.zeros_like(q2)
        qm = jnp.concatenate([jnp.where(left, q2, zero), jnp.where(left, zero, q2)], axis=0)
        s = lax.dot_general(qm, kk, (((1,), (1,)), ((), ())), preferred_element_type=F32)
        outs = []
        for e in range(2):
            hidx = g * rper + 2 * p + e
            slope = c_ref[0, hidx]
            sink = c_ref[1, hidx]
            u = s[e * w:(e + 1) * w] - slope * dist
            m = jnp.maximum(jnp.max(u, axis=-1, keepdims=True), sink)
            pr = jnp.exp(u - m)
            l = jnp.sum(pr, axis=-1, keepdims=True) + jnp.exp(sink - m)
            o = jnp.dot(pr.astype(BF16), vv, preferred_element_type=F32)
            outs.append(o / l)
        o_ref[:, p * LANES:(p + 1) * LANES] = jnp.where(left, outs[0], outs[1]).astype(o_ref.dtype)


def swa_attention(proj, sinks, *, batch, seq):
    t = batch * seq
    w = SWA_WINDOW
    nb = seq // w
    rper = SWA_HEADS // SWA_KV_HEADS
    slopes = 2.0 ** (-8.0 * jnp.arange(1, SWA_HEADS + 1, dtype=F32) / SWA_HEADS)
    consts = jnp.stack([slopes, sinks.astype(F32)])
    tables = _swa_dist_tables()
    qw = rper * SWA_DH
    kcol = C_SK // LANES
    vcol = C_SV // LANES

    def prev(b, n):
        return b * nb + jnp.maximum(n - 1, 0)

    return pl.pallas_call(
        _swa_kernel,
        out_shape=jax.ShapeDtypeStruct((t, SWA_HEADS * SWA_DH), BF16),
        grid=(batch, SWA_KV_HEADS, nb),
        in_specs=[
            pl.BlockSpec(memory_space=pltpu.SMEM),
            pl.BlockSpec((1, w, 2 * w), lambda b, g, n: (jnp.minimum(n, 1), 0, 0)),
            pl.BlockSpec((w, qw), lambda b, g, n: (b * nb + n, C_SQ // qw + g)),
            pl.BlockSpec((w, LANES), lambda b, g, n: (prev(b, n), kcol)),
            pl.BlockSpec((w, LANES), lambda b, g, n: (b * nb + n, kcol)),
            pl.BlockSpec((w, LANES), lambda b, g, n: (prev(b, n), vcol)),
            pl.BlockSpec((w, LANES), lambda b, g, n: (b * nb + n, vcol)),
        ],
        out_specs=pl.BlockSpec((w, qw), lambda b, g, n: (b * nb + n, g)),
        compiler_params=_cparams(("parallel", "parallel", "arbitrary")),
        name="swa_attention",
    )(consts, tables, proj, proj, proj, proj, proj)


def _gla_masks(tb):
    c = GLA_CHUNK
    t = np.arange(tb)[:, None]
    s = np.arange(tb)[None, :]
    same = (t // c) == (s // c)
    tri = same & (s <= t)
    return jnp.asarray(np.concatenate([tri, same], axis=0).astype(np.float32)).astype(BF16)


def _gla_kernel(mask_ref, w2_ref, gb_ref, ng_ref, q_ref, k_ref, v_ref, og_ref, lr_ref, o_ref, st_ref, *, tb):
    c = GLA_CHUNK
    nchunk = tb // c

    @pl.when(pl.program_id(2) == 0)
    def _():
        st_ref[...] = jnp.zeros_like(st_ref)

    z = jnp.dot(lr_ref[...], w2_ref[...], preferred_element_type=F32) + gb_ref[...]
    log_a = (jnp.minimum(z, 0.0) - jnp.log1p(jnp.exp(-jnp.abs(z)))) / GLA_TAU

    hi = log_a.astype(BF16)
    lo = (log_a - hi.astype(F32)).astype(BF16)
    hl = jnp.concatenate([hi, lo], axis=1)
    cs = jnp.dot(mask_ref[...], hl, preferred_element_type=F32)
    b = cs[0:tb, 0:LANES] + cs[0:tb, LANES:2 * LANES]
    b_last = cs[tb:2 * tb, 0:LANES] + cs[tb:2 * tb, LANES:2 * LANES]

    qf = q_ref[...].astype(F32)
    kf = k_ref[...].astype(F32)
    q_dec = (qf * jnp.exp(b)).astype(BF16)
    k_inv = (kf * jnp.exp(-b)).astype(BF16)
    k_end = (kf * jnp.exp(b_last - b)).astype(BF16)
    decay = jnp.exp(b_last)

    lane = lax.broadcasted_iota(jnp.int32, (tb, LANES), 1)
    left = lane < GLA_DK
    zero = jnp.zeros_like(q_dec)
    qd = [jnp.where(left, q_dec, zero), jnp.where(left, zero, q_dec)]
    tri = mask_ref[0:tb, :] > 0
    v = v_ref[...]

    intra = []
    for h in range(2):
        a = lax.dot_general(qd[h], k_inv, (((1,), (1,)), ((), ())), preferred_element_type=F32)
        a = jnp.where(tri, a, 0.0).astype(BF16)
        intra.append(jnp.dot(a, v[:, h * GLA_DV:(h + 1) * GLA_DV], preferred_element_type=F32))

    srow = lax.broadcasted_iota(jnp.int32, (2 * GLA_DV, LANES), 0) // GLA_DV
    scol = lax.broadcasted_iota(jnp.int32, (2 * GLA_DV, LANES), 1) // GLA_DK
    own = srow == scol
    state = st_ref[...]
    inter = []
    for n in range(nchunk):
        r0, r1 = n * c, (n + 1) * c
        inter.append(lax.dot_general(q_dec[r0:r1], state.astype(BF16), (((1,), (1,)), ((), ())),
                                     preferred_element_type=F32))
        kv_t = lax.dot_general(v[r0:r1], k_end[r0:r1], (((0,), (0,)), ((), ())),
                               preferred_element_type=F32)
        state = state * decay[r0:r0 + 1] + jnp.where(own, kv_t, 0.0)
    st_ref[...] = state
    o_inter = jnp.concatenate(inter, axis=0)

    og = og_ref[...].astype(F32)
    gate = og / (1.0 + jnp.exp(-og))
    for h in range(2):
        o = intra[h] + o_inter[:, h * GLA_DV:(h + 1) * GLA_DV]
        ms = jnp.mean(o * o, axis=-1, keepdims=True)
        y = o * lax.rsqrt(ms + EPS) * ng_ref[...] * gate[:, h * GLA_DV:(h + 1) * GLA_DV]
        o_ref[:, h * GLA_DV:(h + 1) * GLA_DV] = y.astype(o_ref.dtype)


def gla_attention(proj, w2p, gate_b, norm_g, *, batch, seq, tb=512):
    tb = min(tb, seq)
    t = batch * seq
    nblk = seq // tb
    masks = _gla_masks(tb)
    kern = functools.partial(_gla_kernel, tb=tb)
    w256 = 2 * GLA_DV
    return pl.pallas_call(
        kern,
        out_shape=jax.ShapeDtypeStruct((t, GLA_HEADS * GLA_DV), BF16),
        grid=(batch, GLA_HEADS // 2, nblk),
        in_specs=[
            pl.BlockSpec((2 * tb, tb), lambda b, h, n: (0, 0)),
            pl.BlockSpec((LANES, LANES), lambda b, h, n: (0, h)),
            pl.BlockSpec((1, LANES), lambda b, h, n: (0, h)),
            pl.BlockSpec((1, GLA_DV), lambda b, h, n: (0, 0)),
            pl.BlockSpec((tb, LANES), lambda b, h, n: (b * nblk + n, C_GQ // LANES + h)),
            pl.BlockSpec((tb, LANES), lambda b, h, n: (b * nblk + n, C_GK // LANES + h)),
            pl.BlockSpec((tb, w256), lambda b, h, n: (b * nblk + n, C_GV // w256 + h)),
            pl.BlockSpec((tb, w256), lambda b, h, n: (b * nblk + n, C_OG // w256 + h)),
            pl.BlockSpec((tb, LANES), lambda b, h, n: (b * nblk + n, C_LR // LANES)),
        ],
        out_specs=pl.BlockSpec((tb, w256), lambda b, h, n: (b * nblk + n, h)),
        scratch_shapes=[pltpu.VMEM((2 * GLA_DV, LANES), F32)],
        compiler_params=_cparams(("parallel", "parallel", "arbitrary")),
        name="gla_attention",
    )(masks, w2p, gate_b.reshape(1, -1).astype(F32), norm_g.reshape(1, -1).astype(F32),
      proj, proj, proj, proj, proj)


def _mix_out_kernel(yd_ref, ys_ref, yg_ref, wd_ref, ws_ref, wg_ref, h_ref, o_ref):
    acc = jnp.dot(yd_ref[...], wd_ref[...], preferred_element_type=F32)
    acc += jnp.dot(ys_ref[...], ws_ref[...], preferred_element_type=F32)
    acc += jnp.dot(yg_ref[...], wg_ref[...], preferred_element_type=F32)
    o_ref[...] = h_ref[...] + acc


def mix_out(yd, ys, yg, wd, ws, wg, h, *, tm):
    m, d = h.shape
    row = lambda i: (i, 0)
    fixed = lambda i: (0, 0)
    return pl.pallas_call(
        _mix_out_kernel,
        out_shape=jax.ShapeDtypeStruct((m, d), F32),
        grid=(m // tm,),
        in_specs=[
            pl.BlockSpec((tm, yd.shape[1]), row),
            pl.BlockSpec((tm, ys.shape[1]), row),
            pl.BlockSpec((tm, yg.shape[1]), row),
            pl.BlockSpec(wd.shape, fixed),
            pl.BlockSpec(ws.shape, fixed),
            pl.BlockSpec(wg.shape, fixed),
            pl.BlockSpec((tm, d), row),
        ],
        out_specs=pl.BlockSpec((tm, d), row),
        compiler_params=_cparams(("parallel",)),
        name="mix_out",
    )(yd, ys, yg, wd, ws, wg, h)


def _xattn_kernel(h_ref, g_ref, wq_ref, kv_ref, wo_ref, gf_ref, o_ref, xn_ref):
    x = h_ref[...]
    ms = jnp.mean(x * x, axis=-1, keepdims=True)
    xn = (x * lax.rsqrt(ms + EPS) * g_ref[...]).astype(BF16)
    q = jnp.dot(xn, wq_ref[...], preferred_element_type=F32).astype(BF16)
    d_xa = XA_HEADS * XA_DH
    outs = []
    for hd in range(XA_HEADS):
        kh = kv_ref[:, hd * XA_DH:(hd + 1) * XA_DH]
        vh = kv_ref[:, d_xa + hd * XA_DH:d_xa + (hd + 1) * XA_DH]
        s = lax.dot_general(q[:, hd * XA_DH:(hd + 1) * XA_DH], kh, (((1,), (1,)), ((), ())),
                            preferred_element_type=F32)
        m = jnp.max(s, axis=-1, keepdims=True)
        p = jnp.exp(s - m)
        l = jnp.sum(p, axis=-1, keepdims=True)
        outs.append((jnp.dot(p.astype(BF16), vh, preferred_element_type=F32) / l).astype(BF16))
    o = jnp.concatenate(outs, axis=1)
    y = x + jnp.dot(o, wo_ref[...], preferred_element_type=F32)
    o_ref[...] = y
    ms2 = jnp.mean(y * y, axis=-1, keepdims=True)
    xn_ref[...] = (y * lax.rsqrt(ms2 + EPS) * gf_ref[...]).astype(BF16)


def cross_attention(h, g, wq, kv, wo, g_ffn, *, batch, seq, n_mem, tm):
    m, d = h.shape
    nt = seq // tm
    fixed = lambda b, i: (0, 0)
    row = lambda b, i: (b * nt + i, 0)
    return pl.pallas_call(
        _xattn_kernel,
        out_shape=(jax.ShapeDtypeStruct((m, d), F32), jax.ShapeDtypeStruct((m, d), BF16)),
        grid=(batch, nt),
        in_specs=[
            pl.BlockSpec((tm, d), row),
            pl.BlockSpec((1, d), fixed),
            pl.BlockSpec(wq.shape, fixed),
            pl.BlockSpec((n_mem, kv.shape[1]), lambda b, i: (b, 0)),
            pl.BlockSpec(wo.shape, fixed),
            pl.BlockSpec((1, d), fixed),
        ],
        out_specs=(pl.BlockSpec((tm, d), row), pl.BlockSpec((tm, d), row)),
        compiler_params=_cparams(("parallel", "arbitrary")),
        name="cross_attention",
    )(h, g.reshape(1, d), wq, kv, wo, g_ffn.reshape(1, d))


def _ffn_up_kernel(x_ref, xp_ref, wg_ref, wu_ref, cg_ref, cu_ref, bg_ref, bu_ref, o_ref, xe_ref, *, tm, tiles_per_seq):
    i = pl.program_id(0)
    hl = CONV_HALO

    @pl.when(pl.program_id(1) == 0)
    def _():
        first = (i % tiles_per_seq) == 0
        prev = xp_ref[...]
        xe_ref[0:hl, :] = jnp.where(first, jnp.zeros_like(prev), prev)
        xe_ref[hl:hl + tm, :] = x_ref[...]

    xe = xe_ref[...]

    def branch(w_ref, c_ref, b_ref):
        hh = jnp.dot(xe, w_ref[...], preferred_element_type=F32)
        cw = c_ref[...]
        return (hh[hl - 2:hl - 2 + tm] * cw[0:1] + hh[hl - 1:hl - 1 + tm] * cw[1:2]
                + hh[hl:hl + tm] * cw[2:3] + b_ref[...])

    gate = branch(wg_ref, cg_ref, bg_ref)
    up = branch(wu_ref, cu_ref, bu_ref)
    o_ref[...] = (gate / (1.0 + jnp.exp(-gate)) * up).astype(o_ref.dtype)


def ffn_up(xn, w_up, conv_w, conv_b, *, seq, tm, tf):
    m, d = xn.shape
    d_ff = w_up.shape[1] // 2
    nf = d_ff // tf
    hl = CONV_HALO
    kern = functools.partial(_ffn_up_kernel, tm=tm, tiles_per_seq=seq // tm)
    return pl.pallas_call(
        kern,
        out_shape=jax.ShapeDtypeStruct((m, d_ff), BF16),
        grid=(m // tm, nf),
        in_specs=[
            pl.BlockSpec((tm, d), lambda i, j: (i, 0)),
            pl.BlockSpec((hl, d), lambda i, j: (jnp.maximum(i * (tm // hl) - 1, 0), 0)),
            pl.BlockSpec((d, tf), lambda i, j: (0, j)),
            pl.BlockSpec((d, tf), lambda i, j: (0, nf + j)),
            pl.BlockSpec((CONV_W, tf), lambda i, j: (0, j)),
            pl.BlockSpec((CONV_W, tf), lambda i, j: (0, nf + j)),
            pl.BlockSpec((1, tf), lambda i, j: (0, j)),
            pl.BlockSpec((1, tf), lambda i, j: (0, nf + j)),
        ],
        out_specs=pl.BlockSpec((tm, tf), lambda i, j: (i, j)),
        scratch_shapes=[pltpu.VMEM((tm + hl, d), BF16)],
        compiler_params=_cparams(("parallel", "arbitrary")),
        name="ffn_up",
    )(xn, xn, w_up, w_up, conv_w, conv_w, conv_b.reshape(1, -1), conv_b.reshape(1, -1))


def _matmul_res_kernel(a_ref, w_ref, r_ref, o_ref):
    o_ref[...] = r_ref[...] + jnp.dot(a_ref[...], w_ref[...], preferred_element_type=F32)


def matmul_residual(a, w, res, *, tm, tn):
    m, k = a.shape
    n = w.shape[1]
    return pl.pallas_call(
        _matmul_res_kernel,
        out_shape=jax.ShapeDtypeStruct((m, n), F32),
        grid=(m // tm, n // tn),
        in_specs=[
            pl.BlockSpec((tm, k), lambda i, j: (i, 0)),
            pl.BlockSpec((k, tn), lambda i, j: (0, j)),
            pl.BlockSpec((tm, tn), lambda i, j: (i, j)),
        ],
        out_specs=pl.BlockSpec((tm, tn), lambda i, j: (i, j)),
        compiler_params=_cparams(("parallel", "arbitrary")),
        name="matmul_residual",
    )(a, w, res)


def _rmsnorm_kernel(x_ref, g_ref, o_ref):
    x = x_ref[...]
    ms = jnp.mean(x * x, axis=-1, keepdims=True)
    o_ref[...] = x * lax.rsqrt(ms + EPS) * g_ref[...]


def rmsnorm(x, g, *, tm):
    m, d = x.shape
    return pl.pallas_call(
        _rmsnorm_kernel,
        out_shape=jax.ShapeDtypeStruct((m, d), F32),
        grid=(m // tm,),
        in_specs=[pl.BlockSpec((tm, d), lambda i: (i, 0)), pl.BlockSpec((1, d), lambda i: (0, 0))],
        out_specs=pl.BlockSpec((tm, d), lambda i: (i, 0)),
        compiler_params=_cparams(("parallel",)),
        name="final_rmsnorm",
    )(x, g.reshape(1, d))


def _prep_w_in(w):
    d = w.shape[0]
    scale = jnp.ones((C_OG,), F32)
    scale = scale.at[C_DQ:C_DK].set(DIFF_DQK ** -0.5)
    scale = scale.at[C_SQ:C_SK].set(SWA_DH ** -0.5)
    scale = scale.at[C_GQ:C_GK].set(GLA_DK ** -0.5)
    lr0 = C_OG
    og0 = lr0 + GLA_RANK
    main = w[:, :C_OG] * scale
    og = w[:, og0:og0 + GLA_HEADS * GLA_DV]
    lr = w[:, lr0:lr0 + GLA_RANK]
    pad = jnp.zeros((d, N_PROJ - C_LR - GLA_RANK), w.dtype)
    return jnp.concatenate([main, og, lr, pad], axis=1).astype(BF16)


def kernel(x, mem, norm_mix_g, w_in, diff_lambda, diff_subln_g, swa_sinks, gla_gate_w2, gla_gate_b, gla_norm_g, w_out, norm_xa_g, norm_mem_g, xa_wq, xa_wkv, xa_wo, norm_ffn_g, ffn_w_up, ffn_conv_w, ffn_conv_b, ffn_w_down, final_norm_g):
    batch, seq, d = x.shape
    n_mem = mem.shape[1]
    depth = w_in.shape[0]
    t = batch * seq
    tm = min(512, seq)

    h = x.reshape(t, d)
    memf = mem.reshape(batch * n_mem, d)
    d_diff = DIFF_HEADS * DIFF_DV
    d_swa = SWA_HEADS * SWA_DH

    for l in range(depth):
        lambda_init = 0.8 - 0.6 * math.exp(-0.3 * l)
        linit = jnp.full((1, 1), lambda_init, F32)

        proj = norm_matmul(h, norm_mix_g[l], _prep_w_in(w_in[l]), tm=tm, tn=1536)
        y_diff = diff_attention(proj, diff_lambda[l], linit, diff_subln_g[l], batch=batch, seq=seq)
        y_swa = swa_attention(proj, swa_sinks[l], batch=batch, seq=seq)
        w2p = jnp.zeros((LANES, GLA_HEADS * GLA_DK), F32).at[:GLA_RANK].set(gla_gate_w2[l]).astype(BF16)
        y_gla = gla_attention(proj, w2p, gla_gate_b[l], gla_norm_g[l], batch=batch, seq=seq)
        wo = w_out[l].astype(BF16)
        h = mix_out(y_diff, y_swa, y_gla, wo[:d_diff], wo[d_diff:d_diff + d_swa], wo[d_diff + d_swa:], h, tm=tm)

        kv = norm_matmul(memf, norm_mem_g[l], xa_wkv[l].astype(BF16), tm=min(512, batch * n_mem), tn=512)
        wq = (xa_wq[l] * (XA_DH ** -0.5)).astype(BF16)
        h, xn = cross_attention(h, norm_xa_g[l], wq, kv, xa_wo[l].astype(BF16), norm_ffn_g[l],
                                batch=batch, seq=seq, n_mem=n_mem, tm=tm)

        act = ffn_up(xn, ffn_w_up[l].astype(BF16), ffn_conv_w[l], ffn_conv_b[l], seq=seq, tm=tm, tf=512)
        h = matmul_residual(act, ffn_w_down[l].astype(BF16), h, tm=tm, tn=512)

    out = rmsnorm(h, final_norm_g, tm=tm)
    return out.reshape(batch, seq, d)
```

```python
import functools
import math

import jax
import jax.numpy as jnp
import numpy as np
from jax import lax
from jax.experimental import pallas as pl
from jax.experimental.pallas import tpu as pltpu

F32 = jnp.float32
BF16 = jnp.bfloat16
EPS = 1e-6
NEG = -1e30

LANES = 128
VMEM_LIMIT = 56 * 1024 * 1024

DIFF_HEADS = 8
DIFF_DQK = 32
DIFF_DV = 64
SWA_HEADS = 16
SWA_KV_HEADS = 2
SWA_DH = 64
SWA_WINDOW = 128
GLA_HEADS = 4
GLA_DK = 64
GLA_DV = 128
GLA_RANK = 16
GLA_TAU = 16.0
GLA_CHUNK = 64
XA_HEADS = 4
XA_DH = 128
CONV_W = 3
CONV_HALO = 16
FFN_CHUNK = 256

C_DQ, C_DK, C_DV = 0, 512, 1024
C_SQ, C_SK, C_SV = 1536, 2560, 2688
C_GQ, C_GK, C_GV = 2816, 3072, 3328
C_OG, C_LR = 3840, 4352
N_PROJ = 4608


def _cparams(sem):
    return pltpu.CompilerParams(dimension_semantics=sem, vmem_limit_bytes=VMEM_LIMIT)


def _norm_matmul_kernel(x_ref, g_ref, w_ref, o_ref, xn_ref):
    @pl.when(pl.program_id(1) == 0)
    def _():
        x = x_ref[...]
        ms = jnp.mean(x * x, axis=-1, keepdims=True)
        xn_ref[...] = (x * lax.rsqrt(ms + EPS) * g_ref[...]).astype(BF16)

    o_ref[...] = jnp.dot(xn_ref[...], w_ref[...], preferred_element_type=F32).astype(o_ref.dtype)


def norm_matmul(x, g, w, *, tm, tn, out_dtype=BF16):
    m, k = x.shape
    n = w.shape[1]
    return pl.pallas_call(
        _norm_matmul_kernel,
        out_shape=jax.ShapeDtypeStruct((m, n), out_dtype),
        grid=(m // tm, n // tn),
        in_specs=[
            pl.BlockSpec((tm, k), lambda i, j: (i, 0)),
            pl.BlockSpec((1, k), lambda i, j: (0, 0)),
            pl.BlockSpec((k, tn), lambda i, j: (0, j)),
        ],
        out_specs=pl.BlockSpec((tm, tn), lambda i, j: (i, j)),
        scratch_shapes=[pltpu.VMEM((tm, k), BF16)],
        compiler_params=_cparams(("parallel", "arbitrary")),
        name="norm_matmul",
    )(x, g.reshape(1, k), w)


def _diff_key_features(seq):
    j = np.arange(seq)
    f = np.zeros((seq, LANES), np.float32)
    for o in (0, DIFF_DV):
        f[:, o] = 1.0
        f[:, o + 1] = 1.0
        f[:, o + 2] = j // LANES
        f[:, o + 3] = j % LANES
    return jnp.asarray(f).astype(BF16)


def _diff_kernel(lam_ref, linit_ref, g_ref, kf_ref, q_ref, k_ref, v_ref, o_ref, sa_ref, sb_ref, m_ref, acc_ref, *, tq, tk):
    hp = pl.program_id(1)
    qi = pl.program_id(2)
    hl = pl.program_id(3)
    rows = 2 * tq
    base = hl * DIFF_DV
    fo = DIFF_DV - base

    lane = lax.broadcasted_iota(jnp.int32, (tq, LANES), 1)
    pos = qi * tq + lax.broadcasted_iota(jnp.int32, (tq, LANES), 0)
    head = (2 * hp + hl + 1).astype(F32)
    slope = jnp.exp2(jnp.zeros((tq, LANES), F32) - head * (8.0 / DIFF_HEADS))
    i_hi = (pos // LANES).astype(F32)
    i_lo = (pos % LANES).astype(F32)
    feat = jnp.where(lane == fo, -slope * LANES * i_hi,
                     jnp.where(lane == fo + 1, -slope * i_lo,
                               jnp.where(lane == fo + 2, slope * LANES,
                                         jnp.where(lane == fo + 3, slope, 0.0)))).astype(BF16)
    q = q_ref[...]
    in_m0 = (lane >= base) & (lane < base + DIFF_DQK)
    in_m1 = (lane >= base + DIFF_DQK) & (lane < base + 2 * DIFF_DQK)
    q2 = jnp.concatenate([jnp.where(in_m0, q, feat), jnp.where(in_m1, q, feat)], axis=0)

    klane = lax.broadcasted_iota(jnp.int32, (tk, LANES), 1)
    mine_k = (klane >= base) & (klane < base + DIFF_DV)

    m_ref[...] = jnp.full_like(m_ref, NEG)
    acc_ref[...] = jnp.zeros_like(acc_ref)

    def qk(kj, s_out):
        start = pl.multiple_of(kj * tk, tk)
        ka = jnp.where(mine_k, k_ref[pl.ds(start, tk), :], kf_ref[pl.ds(start, tk), :])
        s_out[...] = lax.dot_general(q2, ka, (((1,), (1,)), ((), ())), preferred_element_type=F32)

    def softmax_pv(s_in, kj, masked):
        s = s_in[...]
        if masked:
            ii = qi * tq + lax.broadcasted_iota(jnp.int32, (rows, tk), 0) % tq
            jj = kj * tk + lax.broadcasted_iota(jnp.int32, (rows, tk), 1)
            s = jnp.where(ii >= jj, s, NEG)
        m_old = m_ref[...]
        m_new = jnp.maximum(m_old, jnp.max(s, axis=-1, keepdims=True))
        p = jnp.exp(s - jnp.concatenate([m_new] * (tk // LANES), axis=1))
        alpha = jnp.exp(m_old - m_new)
        start = pl.multiple_of(kj * tk, tk)
        va = jnp.where(mine_k, v_ref[pl.ds(start, tk), :], jnp.ones((tk, LANES), BF16))
        acc_ref[...] = alpha * acc_ref[...] + jnp.dot(p.astype(BF16), va, preferred_element_type=F32)
        m_ref[...] = m_new

    nfull = lax.div(qi * tq, tk)
    qk(0, sa_ref)

    def pair(t, carry):
        j = 2 * t
        qk(j + 1, sb_ref)
        softmax_pv(sa_ref, j, False)
        qk(j + 2, sa_ref)
        softmax_pv(sb_ref, j + 1, False)
        return carry

    lax.fori_loop(0, lax.div(nfull, 2), pair, 0)
    odd = lax.rem(nfull, 2) == 1

    @pl.when(odd)
    def _():
        qk(nfull, sb_ref)
        softmax_pv(sa_ref, nfull - 1, False)
        softmax_pv(sb_ref, nfull, True)

    @pl.when(jnp.logical_not(odd))
    def _():
        softmax_pv(sa_ref, nfull, True)

    lamv = lam_ref[...]
    lam1 = jnp.exp(jnp.sum(lamv[0:1] * lamv[1:2], axis=-1, keepdims=True))
    lam2 = jnp.exp(jnp.sum(lamv[2:3] * lamv[3:4], axis=-1, keepdims=True))
    linit = linit_ref[...]
    lam_full = lam1 - lam2 + linit

    acc = acc_ref[...]
    mine = (lane >= base) & (lane < base + DIFF_DV)
    n = acc / pltpu.roll(acc, DIFF_DV, axis=1)
    a = jnp.where(mine, n[0:tq] - lam_full * n[tq:2 * tq], 0.0)
    ms = jnp.sum(a * a, axis=-1, keepdims=True) / DIFF_DV
    y = (a * lax.rsqrt(ms + EPS) * g_ref[...] * (1.0 - linit)).astype(o_ref.dtype)

    @pl.when(hl == 0)
    def _():
        o_ref[...] = y

    @pl.when(hl == 1)
    def _():
        o_ref[...] = jnp.where(mine, y, o_ref[...])


def diff_attention(proj, lam, linit, subln_g, *, batch, seq, tq=256, tk=256):
    t = batch * seq
    nq = seq // tq
    g2 = jnp.concatenate([subln_g, subln_g]).reshape(1, LANES).astype(F32)
    kern = functools.partial(_diff_kernel, tq=tq, tk=tk)
    fixed = lambda b, h, i, e: (0, 0)
    return pl.pallas_call(
        kern,
        out_shape=jax.ShapeDtypeStruct((t, DIFF_HEADS * DIFF_DV), BF16),
        grid=(batch, DIFF_HEADS // 2, nq, 2),
        in_specs=[
            pl.BlockSpec((4, DIFF_DQK), fixed),
            pl.BlockSpec((1, 1), fixed),
            pl.BlockSpec((1, LANES), fixed),
            pl.BlockSpec((seq, LANES), fixed),
            pl.BlockSpec((tq, LANES), lambda b, h, i, e: (b * nq + i, C_DQ // LANES + h)),
            pl.BlockSpec((seq, LANES), lambda b, h, i, e: (b, C_DK // LANES + h)),
            pl.BlockSpec((seq, LANES), lambda b, h, i, e: (b, C_DV // LANES + h)),
        ],
        out_specs=pl.BlockSpec((tq, LANES), lambda b, h, i, e: (b * nq + i, h)),
        scratch_shapes=[
            pltpu.VMEM((2 * tq, tk), F32),
            pltpu.VMEM((2 * tq, tk), F32),
            pltpu.VMEM((2 * tq, LANES), F32),
            pltpu.VMEM((2 * tq, LANES), F32),
        ],
        compiler_params=_cparams(("parallel", "parallel", "arbitrary", "arbitrary")),
        name="diff_attention",
    )(lam.astype(F32), linit, g2, _diff_key_features(seq), proj, proj, proj)


def _swa_dist_tables():
    w = SWA_WINDOW
    i = np.arange(w)[:, None]
    j = np.arange(2 * w)[None, :]
    dist = (i + w - j).astype(np.float32)
    valid = (dist >= 0) & (dist < w)
    general = np.where(valid, dist, 1e30)
    first = np.where(valid & (j >= w), dist, 1e30)
    return jnp.asarray(np.stack([first, general]).astype(np.float32))


def _swa_kernel(c_ref, d_ref, q_ref, kp_ref, kc_ref, vp_ref, vc_ref, o_ref):
    g = pl.program_id(1)
    w = SWA_WINDOW
    rper = SWA_HEADS // SWA_KV_HEADS
    lane = lax.broadcasted_iota(jnp.int32, (2 * w, LANES), 1)
    mine = (lane // SWA_DH) == g

    def dup(prev_ref, cur_ref):
        x = jnp.concatenate([prev_ref[...], cur_ref[...]], axis=0).astype(F32)
        return jnp.where(mine, x, pltpu.roll(x, SWA_DH, axis=1)).astype(BF16)

    kk = dup(kp_ref, kc_ref)
    vv = dup(vp_ref, vc_ref)
    dist = d_ref[0]
    qlane = lax.broadcasted_iota(jnp.int32, (w, LANES), 1)
    left = qlane < SWA_DH

    for p in range(rper // 2):
        q2 = q_ref[:, p * LANES:(p + 1) * LANES]
        zero = jnp.zeros_like(q2)
        qm = jnp.concatenate([jnp.where(left, q2, zero), jnp.where(left, zero, q2)], axis=0)
        s = lax.dot_general(qm, kk, (((1,), (1,)), ((), ())), preferred_element_type=F32)
        outs = []
        for e in range(2):
            hidx = g * rper + 2 * p + e
            slope = c_ref[0, hidx]
            sink = c_ref[1, hidx]
            u = s[e * w:(e + 1) * w] - slope * dist
            m = jnp.maximum(jnp.max(u, axis=-1, keepdims=True), sink)
            pr = jnp.exp(u - m)
            l = jnp.sum(pr, axis=-1, keepdims=True) + jnp.exp(sink - m)
            o = jnp.dot(pr.astype(BF16), vv, preferred_element_type=F32)
            outs.append(o / l)
        o_ref[:, p * LANES:(p + 1) * LANES] = jnp.where(left, outs[0], outs[1]).astype(o_ref.dtype)


def swa_attention(proj, sinks, *, batch, seq):
    t = batch * seq
    w = SWA_WINDOW
    nb = seq // w
    rper = SWA_HEADS // SWA_KV_HEADS
    slopes = 2.0 ** (-8.0 * jnp.arange(1, SWA_HEADS + 1, dtype=F32) / SWA_HEADS)
    consts = jnp.stack([slopes, sinks.astype(F32)])
    tables = _swa_dist_tables()
    qw = rper * SWA_DH
    kcol = C_SK // LANES
    vcol = C_SV // LANES

    def prev(b, n):
        return b * nb + jnp.maximum(n - 1, 0)

    return pl.pallas_call(
        _swa_kernel,
        out_shape=jax.ShapeDtypeStruct((t, SWA_HEADS * SWA_DH), BF16),
        grid=(batch, SWA_KV_HEADS, nb),
        in_specs=[
            pl.BlockSpec(memory_space=pltpu.SMEM),
            pl.BlockSpec((1, w, 2 * w), lambda b, g, n: (jnp.minimum(n, 1), 0, 0)),
            pl.BlockSpec((w, qw), lambda b, g, n: (b * nb + n, C_SQ // qw + g)),
            pl.BlockSpec((w, LANES), lambda b, g, n: (prev(b, n), kcol)),
            pl.BlockSpec((w, LANES), lambda b, g, n: (b * nb + n, kcol)),
            pl.BlockSpec((w, LANES), lambda b, g, n: (prev(b, n), vcol)),
            pl.BlockSpec((w, LANES), lambda b, g, n: (b * nb + n, vcol)),
        ],
        out_specs=pl.BlockSpec((w, qw), lambda b, g, n: (b * nb + n, g)),
        compiler_params=_cparams(("parallel", "parallel", "arbitrary")),
        name="swa_attention",
    )(consts, tables, proj, proj, proj, proj, proj)


def _gla_masks(tb):
    c = GLA_CHUNK
    t = np.arange(tb)[:, None]
    s = np.arange(tb)[None, :]
    same = (t // c) == (s // c)
    tri = same & (s <= t)
    return jnp.asarray(np.concatenate([tri, same], axis=0).astype(np.float32)).astype(BF16)


def _gla_kernel(mask_ref, w2_ref, gb_ref, ng_ref, q_ref, k_ref, v_ref, og_ref, lr_ref, o_ref, st_ref, *, tb):
    c = GLA_CHUNK
    nchunk = tb // c

    @pl.when(pl.program_id(2) == 0)
    def _():
        st_ref[...] = jnp.zeros_like(st_ref)

    z = jnp.dot(lr_ref[...], w2_ref[...], preferred_element_type=F32) + gb_ref[...]
    log_a = (jnp.minimum(z, 0.0) - jnp.log1p(jnp.exp(-jnp.abs(z)))) / GLA_TAU

    hi = log_a.astype(BF16)
    lo = (log_a - hi.astype(F32)).astype(BF16)
    hl = jnp.concatenate([hi, lo], axis=1)
    cs = jnp.dot(mask_ref[...], hl, preferred_element_type=F32)
    b = cs[0:tb, 0:LANES] + cs[0:tb, LANES:2 * LANES]
    b_last = cs[tb:2 * tb, 0:LANES] + cs[tb:2 * tb, LANES:2 * LANES]

    qf = q_ref[...].astype(F32)
    kf = k_ref[...].astype(F32)
    q_dec = (qf * jnp.exp(b)).astype(BF16)
    k_inv = (kf * jnp.exp(-b)).astype(BF16)
    k_end = (kf * jnp.exp(b_last - b)).astype(BF16)
    decay = jnp.exp(b_last)

    lane = lax.broadcasted_iota(jnp.int32, (tb, LANES), 1)
    left = lane < GLA_DK
    zero = jnp.zeros_like(q_dec)
    qd = [jnp.where(left, q_dec, zero), jnp.where(left, zero, q_dec)]
    tri = mask_ref[0:tb, :] > 0
    v = v_ref[...]

    intra = []
    for h in range(2):
        a = lax.dot_general(qd[h], k_inv, (((1,), (1,)), ((), ())), preferred_element_type=F32)
        a = jnp.where(tri, a, 0.0).astype(BF16)
        intra.append(jnp.dot(a, v[:, h * GLA_DV:(h + 1) * GLA_DV], preferred_element_type=F32))

    srow = lax.broadcasted_iota(jnp.int32, (2 * GLA_DV, LANES), 0) // GLA_DV
    scol = lax.broadcasted_iota(jnp.int32, (2 * GLA_DV, LANES), 1) // GLA_DK
    own = srow == scol
    state = st_ref[...]
    inter = []
    for n in range(nchunk):
        r0, r1 = n * c, (n + 1) * c
        inter.append(lax.dot_general(q_dec[r0:r1], state.astype(BF16), (((1,), (1,)), ((), ())),
                                     preferred_element_type=F32))
        kv_t = lax.dot_general(v[r0:r1], k_end[r0:r1], (((0,), (0,)), ((), ())),
                               preferred_element_type=F32)
        state = state * decay[r0:r0 + 1] + jnp.where(own, kv_t, 0.0)
    st_ref[...] = state
    o_inter = jnp.concatenate(inter, axis=0)

    og = og_ref[...].astype(F32)
    gate = og / (1.0 + jnp.exp(-og))
    for h in range(2):
        o = intra[h] + o_inter[:, h * GLA_DV:(h + 1) * GLA_DV]
        ms = jnp.mean(o * o, axis=-1, keepdims=True)
        y = o * lax.rsqrt(ms + EPS) * ng_ref[...] * gate[:, h * GLA_DV:(h + 1) * GLA_DV]
        o_ref[:, h * GLA_DV:(h + 1) * GLA_DV] = y.astype(o_ref.dtype)


def gla_attention(proj, w2p, gate_b, norm_g, *, batch, seq, tb=512):
    tb = min(tb, seq)
    t = batch * seq
    nblk = seq // tb
    masks = _gla_masks(tb)
    kern = functools.partial(_gla_kernel, tb=tb)
    w256 = 2 * GLA_DV
    return pl.pallas_call(
        kern,
        out_shape=jax.ShapeDtypeStruct((t, GLA_HEADS * GLA_DV), BF16),
        grid=(batch, GLA_HEADS // 2, nblk),
        in_specs=[
            pl.BlockSpec((2 * tb, tb), lambda b, h, n: (0, 0)),
            pl.BlockSpec((LANES, LANES), lambda b, h, n: (0, h)),
            pl.BlockSpec((1, LANES), lambda b, h, n: (0, h)),
            pl.BlockSpec((1, GLA_DV), lambda b, h, n: (0, 0)),
            pl.BlockSpec((tb, LANES), lambda b, h, n: (b * nblk + n, C_GQ // LANES + h)),
            pl.BlockSpec((tb, LANES), lambda b, h, n: (b * nblk + n, C_GK // LANES + h)),
            pl.BlockSpec((tb, w256), lambda b, h, n: (b * nblk + n, C_GV // w256 + h)),
            pl.BlockSpec((tb, w256), lambda b, h, n: (b * nblk + n, C_OG // w256 + h)),
            pl.BlockSpec((tb, LANES), lambda b, h, n: (b * nblk + n, C_LR // LANES)),
        ],
        out_specs=pl.BlockSpec((tb, w256), lambda b, h, n: (b * nblk + n, h)),
        scratch_shapes=[pltpu.VMEM((2 * GLA_DV, LANES), F32)],
        compiler_params=_cparams(("parallel", "parallel", "arbitrary")),
        name="gla_attention",
    )(masks, w2p, gate_b.reshape(1, -1).astype(F32), norm_g.reshape(1, -1).astype(F32),
      proj, proj, proj, proj, proj)


def _mix_out_kernel(yd_ref, ys_ref, yg_ref, wd_ref, ws_ref, wg_ref, h_ref, o_ref):
    acc = jnp.dot(yd_ref[...], wd_ref[...], preferred_element_type=F32)
    acc += jnp.dot(ys_ref[...], ws_ref[...], preferred_element_type=F32)
    acc += jnp.dot(yg_ref[...], wg_ref[...], preferred_element_type=F32)
    o_ref[...] = h_ref[...] + acc


def mix_out(yd, ys, yg, wd, ws, wg, h, *, tm):
    m, d = h.shape
    row = lambda i: (i, 0)
    fixed = lambda i: (0, 0)
    return pl.pallas_call(
        _mix_out_kernel,
        out_shape=jax.ShapeDtypeStruct((m, d), F32),
        grid=(m // tm,),
        in_specs=[
            pl.BlockSpec((tm, yd.shape[1]), row),
            pl.BlockSpec((tm, ys.shape[1]), row),
            pl.BlockSpec((tm, yg.shape[1]), row),
            pl.BlockSpec(wd.shape, fixed),
            pl.BlockSpec(ws.shape, fixed),
            pl.BlockSpec(wg.shape, fixed),
            pl.BlockSpec((tm, d), row),
        ],
        out_specs=pl.BlockSpec((tm, d), row),
        compiler_params=_cparams(("parallel",)),
        name="mix_out",
    )(yd, ys, yg, wd, ws, wg, h)


def _xattn_kernel(h_ref, g_ref, wq_ref, kv_ref, wo_ref, gf_ref, o_ref, xn_ref):
    x = h_ref[...]
    ms = jnp.mean(x * x, axis=-1, keepdims=True)
    xn = (x * lax.rsqrt(ms + EPS) * g_ref[...]).astype(BF16)
    q = jnp.dot(xn, wq_ref[...], preferred_element_type=F32).astype(BF16)
    d_xa = XA_HEADS * XA_DH
    outs = []
    for hd in range(XA_HEADS):
        kh = kv_ref[:, hd * XA_DH:(hd + 1) * XA_DH]
        vh = kv_ref[:, d_xa + hd * XA_DH:d_xa + (hd + 1) * XA_DH]
        s = lax.dot_general(q[:, hd * XA_DH:(hd + 1) * XA_DH], kh, (((1,), (1,)), ((), ())),
                            preferred_element_type=F32)
        m = jnp.max(s, axis=-1, keepdims=True)
        p = jnp.exp(s - m)
        l = jnp.sum(p, axis=-1, keepdims=True)
        outs.append((jnp.dot(p.astype(BF16), vh, preferred_element_type=F32) / l).astype(BF16))
    o = jnp.concatenate(outs, axis=1)
    y = x + jnp.dot(o, wo_ref[...], preferred_element_type=F32)
    o_ref[...] = y
    ms2 = jnp.mean(y * y, axis=-1, keepdims=True)
    xn_ref[...] = (y * lax.rsqrt(ms2 + EPS) * gf_ref[...]).astype(BF16)


def cross_attention(h, g, wq, kv, wo, g_ffn, *, batch, seq, n_mem, tm):
    m, d = h.shape
    nt = seq // tm
    fixed = lambda b, i: (0, 0)
    row = lambda b, i: (b * nt + i, 0)
    return pl.pallas_call(
        _xattn_kernel,
        out_shape=(jax.ShapeDtypeStruct((m, d), F32), jax.ShapeDtypeStruct((m, d), BF16)),
        grid=(batch, nt),
        in_specs=[
            pl.BlockSpec((tm, d), row),
            pl.BlockSpec((1, d), fixed),
            pl.BlockSpec(wq.shape, fixed),
            pl.BlockSpec((n_mem, kv.shape[1]), lambda b, i: (b, 0)),
            pl.BlockSpec(wo.shape, fixed),
            pl.BlockSpec((1, d), fixed),
        ],
        out_specs=(pl.BlockSpec((tm, d), row), pl.BlockSpec((tm, d), row)),
        compiler_params=_cparams(("parallel", "arbitrary")),
        name="cross_attention",
    )(h, g.reshape(1, d), wq, kv, wo, g_ffn.reshape(1, d))


def _ffn_up_kernel(x_ref, xp_ref, wg_ref, wu_ref, cg_ref, cu_ref, bg_ref, bu_ref, o_ref, xe_ref, *, tm, tiles_per_seq):
    i = pl.program_id(0)
    hl = CONV_HALO

    @pl.when(pl.program_id(1) == 0)
    def _():
        first = (i % tiles_per_seq) == 0
        prev = xp_ref[...]
        xe_ref[0:hl, :] = jnp.where(first, jnp.zeros_like(prev), prev)
        xe_ref[hl:hl + tm, :] = x_ref[...]

    xe = xe_ref[...]
    tf = o_ref.shape[1]

    def branch(w_ref, c_ref, b_ref, c0, c1):
        hh = jnp.dot(xe, w_ref[:, c0:c1], preferred_element_type=F32)
        cw = c_ref[:, c0:c1]
        return (hh[hl - 2:hl - 2 + tm] * cw[0:1] + hh[hl - 1:hl - 1 + tm] * cw[1:2]
                + hh[hl:hl + tm] * cw[2:3] + b_ref[:, c0:c1])

    for c0 in range(0, tf, FFN_CHUNK):
        c1 = c0 + FFN_CHUNK
        gate = branch(wg_ref, cg_ref, bg_ref, c0, c1)
        up = branch(wu_ref, cu_ref, bu_ref, c0, c1)
        o_ref[:, c0:c1] = (gate / (1.0 + jnp.exp(-gate)) * up).astype(o_ref.dtype)


def ffn_up(xn, w_up, conv_w, conv_b, *, seq, tm, tf):
    m, d = xn.shape
    d_ff = w_up.shape[1] // 2
    nf = d_ff // tf
    hl = CONV_HALO
    kern = functools.partial(_ffn_up_kernel, tm=tm, tiles_per_seq=seq // tm)
    return pl.pallas_call(
        kern,
        out_shape=jax.ShapeDtypeStruct((m, d_ff), BF16),
        grid=(m // tm, nf),
        in_specs=[
            pl.BlockSpec((tm, d), lambda i, j: (i, 0)),
            pl.BlockSpec((hl, d), lambda i, j: (jnp.maximum(i * (tm // hl) - 1, 0), 0)),
            pl.BlockSpec((d, tf), lambda i, j: (0, j)),
            pl.BlockSpec((d, tf), lambda i, j: (0, nf + j)),
            pl.BlockSpec((CONV_W, tf), lambda i, j: (0, j)),
            pl.BlockSpec((CONV_W, tf), lambda i, j: (0, nf + j)),
            pl.BlockSpec((1, tf), lambda i, j: (0, j)),
            pl.BlockSpec((1, tf), lambda i, j: (0, nf + j)),
        ],
        out_specs=pl.BlockSpec((tm, tf), lambda i, j: (i, j)),
        scratch_shapes=[pltpu.VMEM((tm + hl, d), BF16)],
        compiler_params=_cparams(("parallel", "arbitrary")),
        name="ffn_up",
    )(xn, xn, w_up, w_up, conv_w, conv_w, conv_b.reshape(1, -1), conv_b.reshape(1, -1))


def _matmul_res_kernel(a_ref, w_ref, r_ref, o_ref):
    o_ref[...] = r_ref[...] + jnp.dot(a_ref[...], w_ref[...], preferred_element_type=F32)


def matmul_residual(a, w, res, *, tm, tn):
    m, k = a.shape
    n = w.shape[1]
    return pl.pallas_call(
        _matmul_res_kernel,
        out_shape=jax.ShapeDtypeStruct((m, n), F32),
        grid=(m // tm, n // tn),
        in_specs=[
            pl.BlockSpec((tm, k), lambda i, j: (i, 0)),
            pl.BlockSpec((k, tn), lambda i, j: (0, j)),
            pl.BlockSpec((tm, tn), lambda i, j: (i, j)),
        ],
        out_specs=pl.BlockSpec((tm, tn), lambda i, j: (i, j)),
        compiler_params=_cparams(("parallel", "arbitrary")),
        name="matmul_residual",
    )(a, w, res)


def _rmsnorm_kernel(x_ref, g_ref, o_ref):
    x = x_ref[...]
    ms = jnp.mean(x * x, axis=-1, keepdims=True)
    o_ref[...] = x * lax.rsqrt(ms + EPS) * g_ref[...]


def rmsnorm(x, g, *, tm):
    m, d = x.shape
    return pl.pallas_call(
        _rmsnorm_kernel,
        out_shape=jax.ShapeDtypeStruct((m, d), F32),
        grid=(m // tm,),
        in_specs=[pl.BlockSpec((tm, d), lambda i: (i, 0)), pl.BlockSpec((1, d), lambda i: (0, 0))],
        out_specs=pl.BlockSpec((tm, d), lambda i: (i, 0)),
        compiler_params=_cparams(("parallel",)),
        name="final_rmsnorm",
    )(x, g.reshape(1, d))


def _prep_w_in(w):
    d = w.shape[0]
    scale = jnp.ones((C_OG,), F32)
    scale = scale.at[C_DQ:C_DK].set(DIFF_DQK ** -0.5)
    scale = scale.at[C_SQ:C_SK].set(SWA_DH ** -0.5)
    scale = scale.at[C_GQ:C_GK].set(GLA_DK ** -0.5)
    lr0 = C_OG
    og0 = lr0 + GLA_RANK
    main = w[:, :C_OG] * scale
    og = w[:, og0:og0 + GLA_HEADS * GLA_DV]
    lr = w[:, lr0:lr0 + GLA_RANK]
    pad = jnp.zeros((d, N_PROJ - C_LR - GLA_RANK), w.dtype)
    return jnp.concatenate([main, og, lr, pad], axis=1).astype(BF16)


def kernel(x, mem, norm_mix_g, w_in, diff_lambda, diff_subln_g, swa_sinks, gla_gate_w2, gla_gate_b, gla_norm_g, w_out, norm_xa_g, norm_mem_g, xa_wq, xa_wkv, xa_wo, norm_ffn_g, ffn_w_up, ffn_conv_w, ffn_conv_b, ffn_w_down, final_norm_g):
    batch, seq, d = x.shape
    n_mem = mem.shape[1]
    depth = w_in.shape[0]
    t = batch * seq
    tm = min(512, seq)

    h = x.reshape(t, d)
    memf = mem.reshape(batch * n_mem, d)
    d_diff = DIFF_HEADS * DIFF_DV
    d_swa = SWA_HEADS * SWA_DH

    for l in range(depth):
        lambda_init = 0.8 - 0.6 * math.exp(-0.3 * l)
        linit = jnp.full((1, 1), lambda_init, F32)

        proj = norm_matmul(h, norm_mix_g[l], _prep_w_in(w_in[l]), tm=tm, tn=1536)
        y_diff = diff_attention(proj, diff_lambda[l], linit, diff_subln_g[l], batch=batch, seq=seq)
        y_swa = swa_attention(proj, swa_sinks[l], batch=batch, seq=seq)
        w2p = jnp.zeros((LANES, GLA_HEADS * GLA_DK), F32).at[:GLA_RANK].set(gla_gate_w2[l]).astype(BF16)
        y_gla = gla_attention(proj, w2p, gla_gate_b[l], gla_norm_g[l], batch=batch, seq=seq)
        wo = w_out[l].astype(BF16)
        h = mix_out(y_diff, y_swa, y_gla, wo[:d_diff], wo[d_diff:d_diff + d_swa], wo[d_diff + d_swa:], h, tm=tm)

        kv = norm_matmul(memf, norm_mem_g[l], xa_wkv[l].astype(BF16), tm=min(512, batch * n_mem), tn=512)
        wq = (xa_wq[l] * (XA_DH ** -0.5)).astype(BF16)
        h, xn = cross_attention(h, norm_xa_g[l], wq, kv, xa_wo[l].astype(BF16), norm_ffn_g[l],
                                batch=batch, seq=seq, n_mem=n_mem, tm=tm)

        act = ffn_up(xn, ffn_w_up[l].astype(BF16), ffn_conv_w[l], ffn_conv_b[l], seq=seq, tm=min(1024, seq), tf=512)
        h = matmul_residual(act, ffn_w_down[l].astype(BF16), h, tm=tm, tn=512)

    out = rmsnorm(h, final_norm_g, tm=tm)
    return out.reshape(batch, seq, d)
```

```python
import functools
import math

import jax
import jax.numpy as jnp
import numpy as np
from jax import lax
from jax.experimental import pallas as pl
from jax.experimental.pallas import tpu as pltpu

F32 = jnp.float32
BF16 = jnp.bfloat16
EPS = 1e-6
NEG = -1e30

LANES = 128
VMEM_LIMIT = 56 * 1024 * 1024

DIFF_HEADS = 8
DIFF_DQK = 32
DIFF_DV = 64
SWA_HEADS = 16
SWA_KV_HEADS = 2
SWA_DH = 64
SWA_WINDOW = 128
GLA_HEADS = 4
GLA_DK = 64
GLA_DV = 128
GLA_RANK = 16
GLA_TAU = 16.0
GLA_CHUNK = 64
XA_HEADS = 4
XA_DH = 128
CONV_W = 3
CONV_HALO = 16
FFN_CHUNK = 256
FFN_STRIP = 256

C_DQ, C_DK, C_DV = 0, 512, 1024
C_SQ, C_SK, C_SV = 1536, 2560, 2688
C_GQ, C_GK, C_GV = 2816, 3072, 3328
C_OG, C_LR = 3840, 4352
N_PROJ = 4608


def _cparams(sem):
    return pltpu.CompilerParams(dimension_semantics=sem, vmem_limit_bytes=VMEM_LIMIT)


def _norm_matmul_kernel(x_ref, g_ref, w_ref, o_ref, xn_ref):
    @pl.when(pl.program_id(1) == 0)
    def _():
        x = x_ref[...]
        ms = jnp.mean(x * x, axis=-1, keepdims=True)
        xn_ref[...] = (x * lax.rsqrt(ms + EPS) * g_ref[...]).astype(BF16)

    o_ref[...] = jnp.dot(xn_ref[...], w_ref[...], preferred_element_type=F32).astype(o_ref.dtype)


def norm_matmul(x, g, w, *, tm, tn, out_dtype=BF16):
    m, k = x.shape
    n = w.shape[1]
    return pl.pallas_call(
        _norm_matmul_kernel,
        out_shape=jax.ShapeDtypeStruct((m, n), out_dtype),
        grid=(m // tm, n // tn),
        in_specs=[
            pl.BlockSpec((tm, k), lambda i, j: (i, 0)),
            pl.BlockSpec((1, k), lambda i, j: (0, 0)),
            pl.BlockSpec((k, tn), lambda i, j: (0, j)),
        ],
        out_specs=pl.BlockSpec((tm, tn), lambda i, j: (i, j)),
        scratch_shapes=[pltpu.VMEM((tm, k), BF16)],
        compiler_params=_cparams(("parallel", "arbitrary")),
        name="norm_matmul",
    )(x, g.reshape(1, k), w)


def _diff_key_features(seq):
    j = np.arange(seq)
    f = np.zeros((seq, LANES), np.float32)
    for o in (0, DIFF_DV):
        f[:, o] = 1.0
        f[:, o + 1] = 1.0
        f[:, o + 2] = j // LANES
        f[:, o + 3] = j % LANES
    return jnp.asarray(f).astype(BF16)


def _diff_kernel(lam_ref, linit_ref, g_ref, kf_ref, q_ref, k_ref, v_ref, o_ref, sa_ref, sb_ref, m_ref, acc_ref, *, tq, tk):
    hp = pl.program_id(1)
    qi = pl.program_id(2)
    hl = pl.program_id(3)
    rows = 2 * tq
    base = hl * DIFF_DV
    fo = DIFF_DV - base

    lane = lax.broadcasted_iota(jnp.int32, (tq, LANES), 1)
    pos = qi * tq + lax.broadcasted_iota(jnp.int32, (tq, LANES), 0)
    head = (2 * hp + hl + 1).astype(F32)
    slope = jnp.exp2(jnp.zeros((tq, LANES), F32) - head * (8.0 / DIFF_HEADS))
    i_hi = (pos // LANES).astype(F32)
    i_lo = (pos % LANES).astype(F32)
    feat = jnp.where(lane == fo, -slope * LANES * i_hi,
                     jnp.where(lane == fo + 1, -slope * i_lo,
                               jnp.where(lane == fo + 2, slope * LANES,
                                         jnp.where(lane == fo + 3, slope, 0.0)))).astype(BF16)
    q = q_ref[...]
    in_m0 = (lane >= base) & (lane < base + DIFF_DQK)
    in_m1 = (lane >= base + DIFF_DQK) & (lane < base + 2 * DIFF_DQK)
    q2 = jnp.concatenate([jnp.where(in_m0, q, feat), jnp.where(in_m1, q, feat)], axis=0)

    klane = lax.broadcasted_iota(jnp.int32, (tk, LANES), 1)
    mine_k = (klane >= base) & (klane < base + DIFF_DV)

    m_ref[...] = jnp.full_like(m_ref, NEG)
    acc_ref[...] = jnp.zeros_like(acc_ref)

    def qk(kj, s_out):
        start = pl.multiple_of(kj * tk, tk)
        ka = jnp.where(mine_k, k_ref[pl.ds(start, tk), :], kf_ref[pl.ds(start, tk), :])
        s_out[...] = lax.dot_general(q2, ka, (((1,), (1,)), ((), ())), preferred_element_type=F32)

    def softmax_pv(s_in, kj, masked):
        s = s_in[...]
        if masked:
            ii = qi * tq + lax.broadcasted_iota(jnp.int32, (rows, tk), 0) % tq
            jj = kj * tk + lax.broadcasted_iota(jnp.int32, (rows, tk), 1)
            s = jnp.where(ii >= jj, s, NEG)
        m_old = m_ref[...]
        m_new = jnp.maximum(m_old, jnp.max(s, axis=-1, keepdims=True))
        p = jnp.exp(s - jnp.concatenate([m_new] * (tk // LANES), axis=1))
        alpha = jnp.exp(m_old - m_new)
        start = pl.multiple_of(kj * tk, tk)
        va = jnp.where(mine_k, v_ref[pl.ds(start, tk), :], jnp.ones((tk, LANES), BF16))
        acc_ref[...] = alpha * acc_ref[...] + jnp.dot(p.astype(BF16), va, preferred_element_type=F32)
        m_ref[...] = m_new

    nfull = lax.div(qi * tq, tk)
    qk(0, sa_ref)

    def pair(t, carry):
        j = 2 * t
        qk(j + 1, sb_ref)
        softmax_pv(sa_ref, j, False)
        qk(j + 2, sa_ref)
        softmax_pv(sb_ref, j + 1, False)
        return carry

    lax.fori_loop(0, lax.div(nfull, 2), pair, 0)
    odd = lax.rem(nfull, 2) == 1

    @pl.when(odd)
    def _():
        qk(nfull, sb_ref)
        softmax_pv(sa_ref, nfull - 1, False)
        softmax_pv(sb_ref, nfull, True)

    @pl.when(jnp.logical_not(odd))
    def _():
        softmax_pv(sa_ref, nfull, True)

    lamv = lam_ref[...]
    lam1 = jnp.exp(jnp.sum(lamv[0:1] * lamv[1:2], axis=-1, keepdims=True))
    lam2 = jnp.exp(jnp.sum(lamv[2:3] * lamv[3:4], axis=-1, keepdims=True))
    linit = linit_ref[...]
    lam_full = lam1 - lam2 + linit

    acc = acc_ref[...]
    mine = (lane >= base) & (lane < base + DIFF_DV)
    n = acc / pltpu.roll(acc, DIFF_DV, axis=1)
    a = jnp.where(mine, n[0:tq] - lam_full * n[tq:2 * tq], 0.0)
    ms = jnp.sum(a * a, axis=-1, keepdims=True) / DIFF_DV
    y = (a * lax.rsqrt(ms + EPS) * g_ref[...] * (1.0 - linit)).astype(o_ref.dtype)

    @pl.when(hl == 0)
    def _():
        o_ref[...] = y

    @pl.when(hl == 1)
    def _():
        o_ref[...] = jnp.where(mine, y, o_ref[...])


def diff_attention(proj, lam, linit, subln_g, *, batch, seq, tq=256, tk=256):
    t = batch * seq
    nq = seq // tq
    g2 = jnp.concatenate([subln_g, subln_g]).reshape(1, LANES).astype(F32)
    kern = functools.partial(_diff_kernel, tq=tq, tk=tk)
    fixed = lambda b, h, i, e: (0, 0)
    return pl.pallas_call(
        kern,
        out_shape=jax.ShapeDtypeStruct((t, DIFF_HEADS * DIFF_DV), BF16),
        grid=(batch, DIFF_HEADS // 2, nq, 2),
        in_specs=[
            pl.BlockSpec((4, DIFF_DQK), fixed),
            pl.BlockSpec((1, 1), fixed),
            pl.BlockSpec((1, LANES), fixed),
            pl.BlockSpec((seq, LANES), fixed),
            pl.BlockSpec((tq, LANES), lambda b, h, i, e: (b * nq + i, C_DQ // LANES + h)),
            pl.BlockSpec((seq, LANES), lambda b, h, i, e: (b, C_DK // LANES + h)),
            pl.BlockSpec((seq, LANES), lambda b, h, i, e: (b, C_DV // LANES + h)),
        ],
        out_specs=pl.BlockSpec((tq, LANES), lambda b, h, i, e: (b * nq + i, h)),
        scratch_shapes=[
            pltpu.VMEM((2 * tq, tk), F32),
            pltpu.VMEM((2 * tq, tk), F32),
            pltpu.VMEM((2 * tq, LANES), F32),
            pltpu.VMEM((2 * tq, LANES), F32),
        ],
        compiler_params=_cparams(("parallel", "parallel", "arbitrary", "arbitrary")),
        name="diff_attention",
    )(lam.astype(F32), linit, g2, _diff_key_features(seq), proj, proj, proj)


def _swa_dist_tables():
    w = SWA_WINDOW
    i = np.arange(w)[:, None]
    j = np.arange(2 * w)[None, :]
    dist = (i + w - j).astype(np.float32)
    valid = (dist >= 0) & (dist < w)
    general = np.where(valid, dist, 1e30)
    first = np.where(valid & (j >= w), dist, 1e30)
    return jnp.asarray(np.stack([first, general]).astype(np.float32))


def _swa_kernel(c_ref, d_ref, q_ref, kp_ref, kc_ref, vp_ref, vc_ref, o_ref):
    g = pl.program_id(1)
    w = SWA_WINDOW
    rper = SWA_HEADS // SWA_KV_HEADS
    lane = lax.broadcasted_iota(jnp.int32, (2 * w, LANES), 1)
    mine = (lane // SWA_DH) == g

    def dup(prev_ref, cur_ref):
        x = jnp.concatenate([prev_ref[...], cur_ref[...]], axis=0).astype(F32)
        return jnp.where(mine, x, pltpu.roll(x, SWA_DH, axis=1)).astype(BF16)

    kk = dup(kp_ref, kc_ref)
    vv = dup(vp_ref, vc_ref)
    dist = d_ref[0]
    qlane = lax.broadcasted_iota(jnp.int32, (w, LANES), 1)
    left = qlane < SWA_DH

    for p in range(rper // 2):
        q2 = q_ref[:, p * LANES:(p + 1) * LANES]
        zero = jnp.zeros_like(q2)
        qm = jnp.concatenate([jnp.where(left, q2, zero), jnp.where(left, zero, q2)], axis=0)
        s = lax.dot_general(qm, kk, (((1,), (1,)), ((), ())), preferred_element_type=F32)
        outs = []
        for e in range(2):
            hidx = g * rper + 2 * p + e
            slope = c_ref[0, hidx]
            sink = c_ref[1, hidx]
            u = s[e * w:(e + 1) * w] - slope * dist
            m = jnp.maximum(jnp.max(u, axis=-1, keepdims=True), sink)
            pr = jnp.exp(u - m)
            l = jnp.sum(pr, axis=-1, keepdims=True) + jnp.exp(sink - m)
            o = jnp.dot(pr.astype(BF16), vv, preferred_element_type=F32)
            outs.append(o / l)
        o_ref[:, p * LANES:(p + 1) * LANES] = jnp.where(left, outs[0], outs[1]).astype(o_ref.dtype)


def swa_attention(proj, sinks, *, batch, seq):
    t = batch * seq
    w = SWA_WINDOW
    nb = seq // w
    rper = SWA_HEADS // SWA_KV_HEADS
    slopes = 2.0 ** (-8.0 * jnp.arange(1, SWA_HEADS + 1, dtype=F32) / SWA_HEADS)
    consts = jnp.stack([slopes, sinks.astype(F32)])
    tables = _swa_dist_tables()
    qw = rper * SWA_DH
    kcol = C_SK // LANES
    vcol = C_SV // LANES

    def prev(b, n):
        return b * nb + jnp.maximum(n - 1, 0)

    return pl.pallas_call(
        _swa_kernel,
        out_shape=jax.ShapeDtypeStruct((t, SWA_HEADS * SWA_DH), BF16),
        grid=(batch, SWA_KV_HEADS, nb),
        in_specs=[
            pl.BlockSpec(memory_space=pltpu.SMEM),
            pl.BlockSpec((1, w, 2 * w), lambda b, g, n: (jnp.minimum(n, 1), 0, 0)),
            pl.BlockSpec((w, qw), lambda b, g, n: (b * nb + n, C_SQ // qw + g)),
            pl.BlockSpec((w, LANES), lambda b, g, n: (prev(b, n), kcol)),
            pl.BlockSpec((w, LANES), lambda b, g, n: (b * nb + n, kcol)),
            pl.BlockSpec((w, LANES), lambda b, g, n: (prev(b, n), vcol)),
            pl.BlockSpec((w, LANES), lambda b, g, n: (b * nb + n, vcol)),
        ],
        out_specs=pl.BlockSpec((w, qw), lambda b, g, n: (b * nb + n, g)),
        compiler_params=_cparams(("parallel", "parallel", "arbitrary")),
        name="swa_attention",
    )(consts, tables, proj, proj, proj, proj, proj)


def _gla_masks(tb):
    c = GLA_CHUNK
    t = np.arange(tb)[:, None]
    s = np.arange(tb)[None, :]
    same = (t // c) == (s // c)
    tri = same & (s <= t)
    return jnp.asarray(np.concatenate([tri, same], axis=0).astype(np.float32)).astype(BF16)


def _gla_kernel(mask_ref, w2_ref, gb_ref, ng_ref, q_ref, k_ref, v_ref, og_ref, lr_ref, o_ref, st_ref, *, tb):
    c = GLA_CHUNK
    nchunk = tb // c

    @pl.when(pl.program_id(2) == 0)
    def _():
        st_ref[...] = jnp.zeros_like(st_ref)

    z = jnp.dot(lr_ref[...], w2_ref[...], preferred_element_type=F32) + gb_ref[...]
    log_a = (jnp.minimum(z, 0.0) - jnp.log1p(jnp.exp(-jnp.abs(z)))) / GLA_TAU

    hi = log_a.astype(BF16)
    lo = (log_a - hi.astype(F32)).astype(BF16)
    hl = jnp.concatenate([hi, lo], axis=1)
    cs = jnp.dot(mask_ref[...], hl, preferred_element_type=F32)
    b = cs[0:tb, 0:LANES] + cs[0:tb, LANES:2 * LANES]
    b_last = cs[tb:2 * tb, 0:LANES] + cs[tb:2 * tb, LANES:2 * LANES]

    qf = q_ref[...].astype(F32)
    kf = k_ref[...].astype(F32)
    q_dec = (qf * jnp.exp(b)).astype(BF16)
    k_inv = (kf * jnp.exp(-b)).astype(BF16)
    k_end = (kf * jnp.exp(b_last - b)).astype(BF16)
    decay = jnp.exp(b_last)

    lane = lax.broadcasted_iota(jnp.int32, (tb, LANES), 1)
    left = lane < GLA_DK
    zero = jnp.zeros_like(q_dec)
    qd = [jnp.where(left, q_dec, zero), jnp.where(left, zero, q_dec)]
    tri = mask_ref[0:tb, :] > 0
    v = v_ref[...]

    intra = []
    for h in range(2):
        a = lax.dot_general(qd[h], k_inv, (((1,), (1,)), ((), ())), preferred_element_type=F32)
        a = jnp.where(tri, a, 0.0).astype(BF16)
        intra.append(jnp.dot(a, v[:, h * GLA_DV:(h + 1) * GLA_DV], preferred_element_type=F32))

    srow = lax.broadcasted_iota(jnp.int32, (2 * GLA_DV, LANES), 0) // GLA_DV
    scol = lax.broadcasted_iota(jnp.int32, (2 * GLA_DV, LANES), 1) // GLA_DK
    own = srow == scol
    state = st_ref[...]
    inter = []
    for n in range(nchunk):
        r0, r1 = n * c, (n + 1) * c
        inter.append(lax.dot_general(q_dec[r0:r1], state.astype(BF16), (((1,), (1,)), ((), ())),
                                     preferred_element_type=F32))
        kv_t = lax.dot_general(v[r0:r1], k_end[r0:r1], (((0,), (0,)), ((), ())),
                               preferred_element_type=F32)
        state = state * decay[r0:r0 + 1] + jnp.where(own, kv_t, 0.0)
    st_ref[...] = state
    o_inter = jnp.concatenate(inter, axis=0)

    og = og_ref[...].astype(F32)
    gate = og / (1.0 + jnp.exp(-og))
    for h in range(2):
        o = intra[h] + o_inter[:, h * GLA_DV:(h + 1) * GLA_DV]
        ms = jnp.mean(o * o, axis=-1, keepdims=True)
        y = o * lax.rsqrt(ms + EPS) * ng_ref[...] * gate[:, h * GLA_DV:(h + 1) * GLA_DV]
        o_ref[:, h * GLA_DV:(h + 1) * GLA_DV] = y.astype(o_ref.dtype)


def gla_attention(proj, w2p, gate_b, norm_g, *, batch, seq, tb=512):
    tb = min(tb, seq)
    t = batch * seq
    nblk = seq // tb
    masks = _gla_masks(tb)
    kern = functools.partial(_gla_kernel, tb=tb)
    w256 = 2 * GLA_DV
    return pl.pallas_call(
        kern,
        out_shape=jax.ShapeDtypeStruct((t, GLA_HEADS * GLA_DV), BF16),
        grid=(batch, GLA_HEADS // 2, nblk),
        in_specs=[
            pl.BlockSpec((2 * tb, tb), lambda b, h, n: (0, 0)),
            pl.BlockSpec((LANES, LANES), lambda b, h, n: (0, h)),
            pl.BlockSpec((1, LANES), lambda b, h, n: (0, h)),
            pl.BlockSpec((1, GLA_DV), lambda b, h, n: (0, 0)),
            pl.BlockSpec((tb, LANES), lambda b, h, n: (b * nblk + n, C_GQ // LANES + h)),
            pl.BlockSpec((tb, LANES), lambda b, h, n: (b * nblk + n, C_GK // LANES + h)),
            pl.BlockSpec((tb, w256), lambda b, h, n: (b * nblk + n, C_GV // w256 + h)),
            pl.BlockSpec((tb, w256), lambda b, h, n: (b * nblk + n, C_OG // w256 + h)),
            pl.BlockSpec((tb, LANES), lambda b, h, n: (b * nblk + n, C_LR // LANES)),
        ],
        out_specs=pl.BlockSpec((tb, w256), lambda b, h, n: (b * nblk + n, h)),
        scratch_shapes=[pltpu.VMEM((2 * GLA_DV, LANES), F32)],
        compiler_params=_cparams(("parallel", "parallel", "arbitrary")),
        name="gla_attention",
    )(masks, w2p, gate_b.reshape(1, -1).astype(F32), norm_g.reshape(1, -1).astype(F32),
      proj, proj, proj, proj, proj)


def _mix_out_kernel(yd_ref, ys_ref, yg_ref, wd_ref, ws_ref, wg_ref, h_ref, o_ref):
    acc = jnp.dot(yd_ref[...], wd_ref[...], preferred_element_type=F32)
    acc += jnp.dot(ys_ref[...], ws_ref[...], preferred_element_type=F32)
    acc += jnp.dot(yg_ref[...], wg_ref[...], preferred_element_type=F32)
    o_ref[...] = h_ref[...] + acc


def mix_out(yd, ys, yg, wd, ws, wg, h, *, tm):
    m, d = h.shape
    row = lambda i: (i, 0)
    fixed = lambda i: (0, 0)
    return pl.pallas_call(
        _mix_out_kernel,
        out_shape=jax.ShapeDtypeStruct((m, d), F32),
        grid=(m // tm,),
        in_specs=[
            pl.BlockSpec((tm, yd.shape[1]), row),
            pl.BlockSpec((tm, ys.shape[1]), row),
            pl.BlockSpec((tm, yg.shape[1]), row),
            pl.BlockSpec(wd.shape, fixed),
            pl.BlockSpec(ws.shape, fixed),
            pl.BlockSpec(wg.shape, fixed),
            pl.BlockSpec((tm, d), row),
        ],
        out_specs=pl.BlockSpec((tm, d), row),
        compiler_params=_cparams(("parallel",)),
        name="mix_out",
    )(yd, ys, yg, wd, ws, wg, h)


def _xattn_kernel(h_ref, g_ref, wq_ref, kv_ref, wo_ref, gf_ref, o_ref, xn_ref):
    x = h_ref[...]
    ms = jnp.mean(x * x, axis=-1, keepdims=True)
    xn = (x * lax.rsqrt(ms + EPS) * g_ref[...]).astype(BF16)
    q = jnp.dot(xn, wq_ref[...], preferred_element_type=F32).astype(BF16)
    d_xa = XA_HEADS * XA_DH
    outs = []
    for hd in range(XA_HEADS):
        kh = kv_ref[:, hd * XA_DH:(hd + 1) * XA_DH]
        vh = kv_ref[:, d_xa + hd * XA_DH:d_xa + (hd + 1) * XA_DH]
        s = lax.dot_general(q[:, hd * XA_DH:(hd + 1) * XA_DH], kh, (((1,), (1,)), ((), ())),
                            preferred_element_type=F32)
        m = jnp.max(s, axis=-1, keepdims=True)
        p = jnp.exp(s - m)
        l = jnp.sum(p, axis=-1, keepdims=True)
        outs.append((jnp.dot(p.astype(BF16), vh, preferred_element_type=F32) / l).astype(BF16))
    o = jnp.concatenate(outs, axis=1)
    y = x + jnp.dot(o, wo_ref[...], preferred_element_type=F32)
    o_ref[...] = y
    ms2 = jnp.mean(y * y, axis=-1, keepdims=True)
    xn_ref[...] = (y * lax.rsqrt(ms2 + EPS) * gf_ref[...]).astype(BF16)


def cross_attention(h, g, wq, kv, wo, g_ffn, *, batch, seq, n_mem, tm):
    m, d = h.shape
    nt = seq // tm
    fixed = lambda b, i: (0, 0)
    row = lambda b, i: (b * nt + i, 0)
    return pl.pallas_call(
        _xattn_kernel,
        out_shape=(jax.ShapeDtypeStruct((m, d), F32), jax.ShapeDtypeStruct((m, d), BF16)),
        grid=(batch, nt),
        in_specs=[
            pl.BlockSpec((tm, d), row),
            pl.BlockSpec((1, d), fixed),
            pl.BlockSpec(wq.shape, fixed),
            pl.BlockSpec((n_mem, kv.shape[1]), lambda b, i: (b, 0)),
            pl.BlockSpec(wo.shape, fixed),
            pl.BlockSpec((1, d), fixed),
        ],
        out_specs=(pl.BlockSpec((tm, d), row), pl.BlockSpec((tm, d), row)),
        compiler_params=_cparams(("parallel", "arbitrary")),
        name="cross_attention",
    )(h, g.reshape(1, d), wq, kv, wo, g_ffn.reshape(1, d))


def _ffn_up_kernel(x_ref, xp_ref, wg_ref, wu_ref, cw_ref, cb_ref, o_ref, xe_ref, ga_ref, ua_ref, gb_ref, ub_ref,
                   *, tm, tiles_per_seq, nf, d_ff):
    i = pl.program_id(0)
    j = pl.program_id(1)
    hl = CONV_HALO
    ck = FFN_CHUNK

    @pl.when(j == 0)
    def _():
        first = (i % tiles_per_seq) == 0
        prev = xp_ref[...]
        xe_ref[0:hl, :] = jnp.where(first, jnp.zeros_like(prev), prev)
        xe_ref[hl:hl + tm, :] = x_ref[...]
        gb_ref[...] = jnp.zeros_like(gb_ref)
        ub_ref[...] = jnp.zeros_like(ub_ref)

    ts = FFN_STRIP
    nstrip = tm // ts

    def matmuls(c0, g_out, u_out, k):
        r0 = 0 if k == 0 else hl + k * ts
        r1 = hl + (k + 1) * ts
        xs = xe_ref[r0:r1, :]
        g_out[r0:r1, :] = jnp.dot(xs, wg_ref[:, c0:c0 + ck], preferred_element_type=F32)
        u_out[r0:r1, :] = jnp.dot(xs, wu_ref[:, c0:c0 + ck], preferred_element_type=F32)

    def conv(h_ref, col, k):
        hh = h_ref[k * ts:k * ts + hl + ts, :]
        cw = cw_ref[:, pl.ds(col, ck)]
        return (hh[hl - 2:hl - 2 + ts] * cw[0:1] + hh[hl - 1:hl - 1 + ts] * cw[1:2]
                + hh[hl:hl + ts] * cw[2:3] + cb_ref[:, pl.ds(col, ck)])

    def epilogue(g_in, u_in, col, k):
        col = pl.multiple_of(col, ck)
        gate = conv(g_in, col, k)
        up = conv(u_in, pl.multiple_of(d_ff + col, ck), k)
        o_ref[k * ts:(k + 1) * ts, pl.ds(col, ck)] = (gate / (1.0 + jnp.exp(-gate)) * up).astype(o_ref.dtype)

    col_a = j * (2 * ck)
    col_prev = jnp.maximum(col_a - ck, 0)
    for k in range(nstrip):
        matmuls(0, ga_ref, ua_ref, k)
        epilogue(gb_ref, ub_ref, col_prev, k)
    for k in range(nstrip):
        matmuls(ck, gb_ref, ub_ref, k)
        epilogue(ga_ref, ua_ref, col_a, k)

    @pl.when(j == nf - 1)
    def _():
        for k in range(nstrip):
            epilogue(gb_ref, ub_ref, col_a + ck, k)


def ffn_up(xn, w_up, conv_w, conv_b, *, seq, tm):
    m, d = xn.shape
    d_ff = w_up.shape[1] // 2
    tf = 2 * FFN_CHUNK
    nf = d_ff // tf
    hl = CONV_HALO
    kern = functools.partial(_ffn_up_kernel, tm=tm, tiles_per_seq=seq // tm, nf=nf, d_ff=d_ff)
    raw = pltpu.VMEM((tm + hl, FFN_CHUNK), F32)
    return pl.pallas_call(
        kern,
        out_shape=jax.ShapeDtypeStruct((m, d_ff), BF16),
        grid=(m // tm, nf),
        in_specs=[
            pl.BlockSpec((tm, d), lambda i, j: (i, 0)),
            pl.BlockSpec((hl, d), lambda i, j: (jnp.maximum(i * (tm // hl) - 1, 0), 0)),
            pl.BlockSpec((d, tf), lambda i, j: (0, j)),
            pl.BlockSpec((d, tf), lambda i, j: (0, nf + j)),
            pl.BlockSpec((CONV_W, 2 * d_ff), lambda i, j: (0, 0)),
            pl.BlockSpec((1, 2 * d_ff), lambda i, j: (0, 0)),
        ],
        out_specs=pl.BlockSpec((tm, d_ff), lambda i, j: (i, 0)),
        scratch_shapes=[pltpu.VMEM((tm + hl, d), BF16), raw, raw, raw, raw],
        compiler_params=_cparams(("parallel", "arbitrary")),
        name="ffn_up",
    )(xn, xn, w_up, w_up, conv_w, conv_b.reshape(1, -1))


def _matmul_res_kernel(a_ref, w_ref, r_ref, o_ref):
    o_ref[...] = r_ref[...] + jnp.dot(a_ref[...], w_ref[...], preferred_element_type=F32)


def matmul_residual(a, w, res, *, tm, tn):
    m, k = a.shape
    n = w.shape[1]
    return pl.pallas_call(
        _matmul_res_kernel,
        out_shape=jax.ShapeDtypeStruct((m, n), F32),
        grid=(m // tm, n // tn),
        in_specs=[
            pl.BlockSpec((tm, k), lambda i, j: (i, 0)),
            pl.BlockSpec((k, tn), lambda i, j: (0, j)),
            pl.BlockSpec((tm, tn), lambda i, j: (i, j)),
        ],
        out_specs=pl.BlockSpec((tm, tn), lambda i, j: (i, j)),
        compiler_params=_cparams(("parallel", "arbitrary")),
        name="matmul_residual",
    )(a, w, res)


def _rmsnorm_kernel(x_ref, g_ref, o_ref):
    x = x_ref[...]
    ms = jnp.mean(x * x, axis=-1, keepdims=True)
    o_ref[...] = x * lax.rsqrt(ms + EPS) * g_ref[...]


def rmsnorm(x, g, *, tm):
    m, d = x.shape
    return pl.pallas_call(
        _rmsnorm_kernel,
        out_shape=jax.ShapeDtypeStruct((m, d), F32),
        grid=(m // tm,),
        in_specs=[pl.BlockSpec((tm, d), lambda i: (i, 0)), pl.BlockSpec((1, d), lambda i: (0, 0))],
        out_specs=pl.BlockSpec((tm, d), lambda i: (i, 0)),
        compiler_params=_cparams(("parallel",)),
        name="final_rmsnorm",
    )(x, g.reshape(1, d))


def _prep_w_in(w):
    d = w.shape[0]
    scale = jnp.ones((C_OG,), F32)
    scale = scale.at[C_DQ:C_DK].set(DIFF_DQK ** -0.5)
    scale = scale.at[C_SQ:C_SK].set(SWA_DH ** -0.5)
    scale = scale.at[C_GQ:C_GK].set(GLA_DK ** -0.5)
    lr0 = C_OG
    og0 = lr0 + GLA_RANK
    main = w[:, :C_OG] * scale
    og = w[:, og0:og0 + GLA_HEADS * GLA_DV]
    lr = w[:, lr0:lr0 + GLA_RANK]
    pad = jnp.zeros((d, N_PROJ - C_LR - GLA_RANK), w.dtype)
    return jnp.concatenate([main, og, lr, pad], axis=1).astype(BF16)


def kernel(x, mem, norm_mix_g, w_in, diff_lambda, diff_subln_g, swa_sinks, gla_gate_w2, gla_gate_b, gla_norm_g, w_out, norm_xa_g, norm_mem_g, xa_wq, xa_wkv, xa_wo, norm_ffn_g, ffn_w_up, ffn_conv_w, ffn_conv_b, ffn_w_down, final_norm_g):
    batch, seq, d = x.shape
    n_mem = mem.shape[1]
    depth = w_in.shape[0]
    t = batch * seq
    tm = min(512, seq)
    tm_big = min(1024, seq)

    h = x.reshape(t, d)
    memf = mem.reshape(batch * n_mem, d)
    d_diff = DIFF_HEADS * DIFF_DV
    d_swa = SWA_HEADS * SWA_DH

    for l in range(depth):
        lambda_init = 0.8 - 0.6 * math.exp(-0.3 * l)
        linit = jnp.full((1, 1), lambda_init, F32)

        proj = norm_matmul(h, norm_mix_g[l], _prep_w_in(w_in[l]), tm=tm_big, tn=1536)
        y_diff = diff_attention(proj, diff_lambda[l], linit, diff_subln_g[l], batch=batch, seq=seq)
        y_swa = swa_attention(proj, swa_sinks[l], batch=batch, seq=seq)
        w2p = jnp.zeros((LANES, GLA_HEADS * GLA_DK), F32).at[:GLA_RANK].set(gla_gate_w2[l]).astype(BF16)
        y_gla = gla_attention(proj, w2p, gla_gate_b[l], gla_norm_g[l], batch=batch, seq=seq)
        wo = w_out[l].astype(BF16)
        h = mix_out(y_diff, y_swa, y_gla, wo[:d_diff], wo[d_diff:d_diff + d_swa], wo[d_diff + d_swa:], h, tm=tm)

        kv = norm_matmul(memf, norm_mem_g[l], xa_wkv[l].astype(BF16), tm=min(512, batch * n_mem), tn=512)
        wq = (xa_wq[l] * (XA_DH ** -0.5)).astype(BF16)
        h, xn = cross_attention(h, norm_xa_g[l], wq, kv, xa_wo[l].astype(BF16), norm_ffn_g[l],
                                batch=batch, seq=seq, n_mem=n_mem, tm=tm)

        act = ffn_up(xn, ffn_w_up[l].astype(BF16), ffn_conv_w[l], ffn_conv_b[l], seq=seq, tm=tm_big)
        h = matmul_residual(act, ffn_w_down[l].astype(BF16), h, tm=tm_big, tn=512)

    out = rmsnorm(h, final_norm_g, tm=tm)
    return out.reshape(batch, seq, d)
```

```python
import functools
import math

import jax
import jax.numpy as jnp
import numpy as np
from jax import lax
from jax.experimental import pallas as pl
from jax.experimental.pallas import tpu as pltpu

F32 = jnp.float32
BF16 = jnp.bfloat16
EPS = 1e-6
NEG = -1e30

LANES = 128
VMEM_LIMIT = 56 * 1024 * 1024

DIFF_HEADS = 8
DIFF_DQK = 32
DIFF_DV = 64
SWA_HEADS = 16
SWA_KV_HEADS = 2
SWA_DH = 64
SWA_WINDOW = 128
GLA_HEADS = 4
GLA_DK = 64
GLA_DV = 128
GLA_RANK = 16
GLA_TAU = 16.0
GLA_CHUNK = 64
XA_HEADS = 4
XA_DH = 128
CONV_W = 3
CONV_HALO = 16
FFN_CHUNK = 256
FFN_STRIP = 256

C_DQ, C_DK, C_DV = 0, 512, 1024
C_SQ, C_SK, C_SV = 1536, 2560, 2688
C_GQ, C_GK, C_GV = 2816, 3072, 3328
C_OG, C_LR = 3840, 4352
N_PROJ = 4608


def _cparams(sem):
    return pltpu.CompilerParams(dimension_semantics=sem, vmem_limit_bytes=VMEM_LIMIT)


def _norm_matmul_kernel(x_ref, g_ref, w_ref, o_ref, xn_ref):
    @pl.when(pl.program_id(1) == 0)
    def _():
        x = x_ref[...]
        ms = jnp.mean(x * x, axis=-1, keepdims=True)
        xn_ref[...] = (x * lax.rsqrt(ms + EPS) * g_ref[...]).astype(BF16)

    o_ref[...] = jnp.dot(xn_ref[...], w_ref[...], preferred_element_type=F32).astype(o_ref.dtype)


def norm_matmul(x, g, w, *, tm, tn, out_dtype=BF16):
    m, k = x.shape
    n = w.shape[1]
    return pl.pallas_call(
        _norm_matmul_kernel,
        out_shape=jax.ShapeDtypeStruct((m, n), out_dtype),
        grid=(m // tm, n // tn),
        in_specs=[
            pl.BlockSpec((tm, k), lambda i, j: (i, 0)),
            pl.BlockSpec((1, k), lambda i, j: (0, 0)),
            pl.BlockSpec((k, tn), lambda i, j: (0, j)),
        ],
        out_specs=pl.BlockSpec((tm, tn), lambda i, j: (i, j)),
        scratch_shapes=[pltpu.VMEM((tm, k), BF16)],
        compiler_params=_cparams(("parallel", "arbitrary")),
        name="norm_matmul",
    )(x, g.reshape(1, k), w)


def _diff_key_features(seq):
    j = np.arange(seq)
    f = np.zeros((seq, LANES), np.float32)
    f[:, 0] = 1.0
    f[:, 1] = 1.0
    f[:, 2] = j // LANES
    f[:, 3] = j % LANES
    return jnp.asarray(f).astype(BF16)


def _diff_kernel(lam_ref, linit_ref, g_ref, kf_ref, q_ref, k_ref, v_ref, o_ref, sa_ref, sb_ref, m_ref, acc_ref, *, tq, tk):
    hp = pl.program_id(1)
    qi = pl.program_id(2)
    rows = 4 * tq

    lane = lax.broadcasted_iota(jnp.int32, (tq, LANES), 1)
    pos = qi * tq + lax.broadcasted_iota(jnp.int32, (tq, LANES), 0)
    i_hi = (pos // LANES).astype(F32)
    i_lo = (pos % LANES).astype(F32)

    def features(hl):
        head = (2 * hp + hl + 1).astype(F32)
        slope = jnp.exp2(jnp.zeros((tq, LANES), F32) - head * (8.0 / DIFF_HEADS))
        return jnp.where(lane == 0, -slope * LANES * i_hi,
                         jnp.where(lane == 1, -slope * i_lo,
                                   jnp.where(lane == 2, slope * LANES,
                                             jnp.where(lane == 3, slope, 0.0)))).astype(BF16)

    q = q_ref[...]
    zero = jnp.zeros_like(q)
    feats = [features(0), features(1)]
    q4 = jnp.concatenate(
        [jnp.concatenate([jnp.where((lane // DIFF_DQK) == c, q, zero), feats[c // 2]], axis=1) for c in range(4)],
        axis=0)

    m_ref[...] = jnp.full_like(m_ref, NEG)
    acc_ref[...] = jnp.zeros_like(acc_ref)
    ones = jnp.ones((tk, LANES), BF16)

    def qk(kj, s_out):
        start = pl.multiple_of(kj * tk, tk)
        ka = jnp.concatenate([k_ref[pl.ds(start, tk), :], kf_ref[pl.ds(start, tk), :]], axis=1)
        s_out[...] = lax.dot_general(q4, ka, (((1,), (1,)), ((), ())), preferred_element_type=F32)

    def softmax_pv(s_in, kj, masked):
        s = s_in[...]
        if masked:
            ii = qi * tq + lax.broadcasted_iota(jnp.int32, (rows, tk), 0) % tq
            jj = kj * tk + lax.broadcasted_iota(jnp.int32, (rows, tk), 1)
            s = jnp.where(ii >= jj, s, NEG)
        m_old = m_ref[...]
        m_new = jnp.maximum(m_old, jnp.max(s, axis=-1, keepdims=True))
        p = jnp.exp(s - jnp.concatenate([m_new] * (tk // LANES), axis=1))
        alpha = jnp.exp(m_old - m_new)
        start = pl.multiple_of(kj * tk, tk)
        va = jnp.concatenate([v_ref[pl.ds(start, tk), :], ones], axis=1)
        acc_ref[...] = (jnp.concatenate([alpha, alpha], axis=1) * acc_ref[...]
                        + jnp.dot(p.astype(BF16), va, preferred_element_type=F32))
        m_ref[...] = m_new

    nfull = lax.div(qi * tq, tk)
    qk(0, sa_ref)

    def pair(t, carry):
        j = 2 * t
        qk(j + 1, sb_ref)
        softmax_pv(sa_ref, j, False)
        qk(j + 2, sa_ref)
        softmax_pv(sb_ref, j + 1, False)
        return carry

    lax.fori_loop(0, lax.div(nfull, 2), pair, 0)
    odd = lax.rem(nfull, 2) == 1

    @pl.when(odd)
    def _():
        qk(nfull, sb_ref)
        softmax_pv(sa_ref, nfull - 1, False)
        softmax_pv(sb_ref, nfull, True)

    @pl.when(jnp.logical_not(odd))
    def _():
        softmax_pv(sa_ref, nfull, True)

    lamv = lam_ref[...]
    lam1 = jnp.exp(jnp.sum(lamv[0:1] * lamv[1:2], axis=-1, keepdims=True))
    lam2 = jnp.exp(jnp.sum(lamv[2:3] * lamv[3:4], axis=-1, keepdims=True))
    linit = linit_ref[...]
    lam_full = lam1 - lam2 + linit

    acc = acc_ref[...]
    n = acc[:, 0:LANES] / acc[:, LANES:2 * LANES]
    a0 = n[0:tq] - lam_full * n[tq:2 * tq]
    a1 = n[2 * tq:3 * tq] - lam_full * n[3 * tq:4 * tq]
    left = lane < DIFF_DV
    o = jnp.where(left, a0, a1)
    sq = o * o
    ms0 = jnp.sum(jnp.where(left, sq, 0.0), axis=-1, keepdims=True) / DIFF_DV
    ms1 = jnp.sum(jnp.where(left, 0.0, sq), axis=-1, keepdims=True) / DIFF_DV
    ms = jnp.where(left, ms0, ms1)
    o_ref[...] = (o * lax.rsqrt(ms + EPS) * g_ref[...] * (1.0 - linit)).astype(o_ref.dtype)


def diff_attention(proj, lam, linit, subln_g, *, batch, seq, tq=256, tk=256):
    t = batch * seq
    nq = seq // tq
    g2 = jnp.concatenate([subln_g, subln_g]).reshape(1, LANES).astype(F32)
    kern = functools.partial(_diff_kernel, tq=tq, tk=tk)
    fixed = lambda b, h, i: (0, 0)
    return pl.pallas_call(
        kern,
        out_shape=jax.ShapeDtypeStruct((t, DIFF_HEADS * DIFF_DV), BF16),
        grid=(batch, DIFF_HEADS // 2, nq),
        in_specs=[
            pl.BlockSpec((4, DIFF_DQK), fixed),
            pl.BlockSpec((1, 1), fixed),
            pl.BlockSpec((1, LANES), fixed),
            pl.BlockSpec((seq, LANES), fixed),
            pl.BlockSpec((tq, LANES), lambda b, h, i: (b * nq + i, C_DQ // LANES + h)),
            pl.BlockSpec((seq, LANES), lambda b, h, i: (b, C_DK // LANES + h)),
            pl.BlockSpec((seq, LANES), lambda b, h, i: (b, C_DV // LANES + h)),
        ],
        out_specs=pl.BlockSpec((tq, LANES), lambda b, h, i: (b * nq + i, h)),
        scratch_shapes=[
            pltpu.VMEM((4 * tq, tk), F32),
            pltpu.VMEM((4 * tq, tk), F32),
            pltpu.VMEM((4 * tq, LANES), F32),
            pltpu.VMEM((4 * tq, 2 * LANES), F32),
        ],
        compiler_params=_cparams(("parallel", "parallel", "arbitrary")),
        name="diff_attention",
    )(lam.astype(F32), linit, g2, _diff_key_features(seq), proj, proj, proj)


def _swa_dist_tables():
    w = SWA_WINDOW
    i = np.arange(w)[:, None]
    j = np.arange(2 * w)[None, :]
    dist = (i + w - j).astype(np.float32)
    valid = (dist >= 0) & (dist < w)
    general = np.where(valid, dist, 1e30)
    first = np.where(valid & (j >= w), dist, 1e30)
    return jnp.asarray(np.stack([first, general]).astype(np.float32))


def _swa_kernel(c_ref, d_ref, q_ref, kp_ref, kc_ref, vp_ref, vc_ref, o_ref):
    g = pl.program_id(1)
    w = SWA_WINDOW
    rper = SWA_HEADS // SWA_KV_HEADS
    lane = lax.broadcasted_iota(jnp.int32, (2 * w, LANES), 1)
    mine = (lane // SWA_DH) == g

    def dup(prev_ref, cur_ref):
        x = jnp.concatenate([prev_ref[...], cur_ref[...]], axis=0).astype(F32)
        return jnp.where(mine, x, pltpu.roll(x, SWA_DH, axis=1)).astype(BF16)

    kk = dup(kp_ref, kc_ref)
    vv = dup(vp_ref, vc_ref)
    dist = d_ref[0]
    qlane = lax.broadcasted_iota(jnp.int32, (w, LANES), 1)
    left = qlane < SWA_DH

    for p in range(rper // 2):
        q2 = q_ref[:, p * LANES:(p + 1) * LANES]
        zero = jnp.zeros_like(q2)
        qm = jnp.concatenate([jnp.where(left, q2, zero), jnp.where(left, zero, q2)], axis=0)
        s = lax.dot_general(qm, kk, (((1,), (1,)), ((), ())), preferred_element_type=F32)
        outs = []
        for e in range(2):
            hidx = g * rper + 2 * p + e
            slope = c_ref[0, hidx]
            sink = c_ref[1, hidx]
            u = s[e * w:(e + 1) * w] - slope * dist
            m = jnp.maximum(jnp.max(u, axis=-1, keepdims=True), sink)
            pr = jnp.exp(u - m)
            l = jnp.sum(pr, axis=-1, keepdims=True) + jnp.exp(sink - m)
            o = jnp.dot(pr.astype(BF16), vv, preferred_element_type=F32)
            outs.append(o / l)
        o_ref[:, p * LANES:(p + 1) * LANES] = jnp.where(left, outs[0], outs[1]).astype(o_ref.dtype)


def swa_attention(proj, sinks, *, batch, seq):
    t = batch * seq
    w = SWA_WINDOW
    nb = seq // w
    rper = SWA_HEADS // SWA_KV_HEADS
    slopes = 2.0 ** (-8.0 * jnp.arange(1, SWA_HEADS + 1, dtype=F32) / SWA_HEADS)
    consts = jnp.stack([slopes, sinks.astype(F32)])
    tables = _swa_dist_tables()
    qw = rper * SWA_DH
    kcol = C_SK // LANES
    vcol = C_SV // LANES

    def prev(b, n):
        return b * nb + jnp.maximum(n - 1, 0)

    return pl.pallas_call(
        _swa_kernel,
        out_shape=jax.ShapeDtypeStruct((t, SWA_HEADS * SWA_DH), BF16),
        grid=(batch, SWA_KV_HEADS, nb),
        in_specs=[
            pl.BlockSpec(memory_space=pltpu.SMEM),
            pl.BlockSpec((1, w, 2 * w), lambda b, g, n: (jnp.minimum(n, 1), 0, 0)),
            pl.BlockSpec((w, qw), lambda b, g, n: (b * nb + n, C_SQ // qw + g)),
            pl.BlockSpec((w, LANES), lambda b, g, n: (prev(b, n), kcol)),
            pl.BlockSpec((w, LANES), lambda b, g, n: (b * nb + n, kcol)),
            pl.BlockSpec((w, LANES), lambda b, g, n: (prev(b, n), vcol)),
            pl.BlockSpec((w, LANES), lambda b, g, n: (b * nb + n, vcol)),
        ],
        out_specs=pl.BlockSpec((w, qw), lambda b, g, n: (b * nb + n, g)),
        compiler_params=_cparams(("parallel", "parallel", "arbitrary")),
        name="swa_attention",
    )(consts, tables, proj, proj, proj, proj, proj)


def _gla_masks(tb):
    c = GLA_CHUNK
    t = np.arange(tb)[:, None]
    s = np.arange(tb)[None, :]
    same = (t // c) == (s // c)
    tri = same & (s <= t)
    return jnp.asarray(np.concatenate([tri, same], axis=0).astype(np.float32)).astype(BF16)


def _gla_kernel(mask_ref, w2_ref, gb_ref, ng_ref, q_ref, k_ref, v_ref, og_ref, lr_ref, o_ref, st_ref, *, tb):
    c = GLA_CHUNK
    nchunk = tb // c

    @pl.when(pl.program_id(2) == 0)
    def _():
        st_ref[...] = jnp.zeros_like(st_ref)

    z = jnp.dot(lr_ref[...], w2_ref[...], preferred_element_type=F32) + gb_ref[...]
    log_a = (jnp.minimum(z, 0.0) - jnp.log1p(jnp.exp(-jnp.abs(z)))) / GLA_TAU

    hi = log_a.astype(BF16)
    lo = (log_a - hi.astype(F32)).astype(BF16)
    hl = jnp.concatenate([hi, lo], axis=1)
    cs = jnp.dot(mask_ref[...], hl, preferred_element_type=F32)
    b = cs[0:tb, 0:LANES] + cs[0:tb, LANES:2 * LANES]
    b_last = cs[tb:2 * tb, 0:LANES] + cs[tb:2 * tb, LANES:2 * LANES]

    qf = q_ref[...].astype(F32)
    kf = k_ref[...].astype(F32)
    q_dec = (qf * jnp.exp(b)).astype(BF16)
    k_inv = (kf * jnp.exp(-b)).astype(BF16)
    k_end = (kf * jnp.exp(b_last - b)).astype(BF16)
    decay = jnp.exp(b_last)

    lane = lax.broadcasted_iota(jnp.int32, (tb, LANES), 1)
    left = lane < GLA_DK
    zero = jnp.zeros_like(q_dec)
    qd = [jnp.where(left, q_dec, zero), jnp.where(left, zero, q_dec)]
    tri = mask_ref[0:tb, :] > 0
    v = v_ref[...]

    intra = []
    for h in range(2):
        a = lax.dot_general(qd[h], k_inv, (((1,), (1,)), ((), ())), preferred_element_type=F32)
        a = jnp.where(tri, a, 0.0).astype(BF16)
        intra.append(jnp.dot(a, v[:, h * GLA_DV:(h + 1) * GLA_DV], preferred_element_type=F32))

    srow = lax.broadcasted_iota(jnp.int32, (2 * GLA_DV, LANES), 0) // GLA_DV
    scol = lax.broadcasted_iota(jnp.int32, (2 * GLA_DV, LANES), 1) // GLA_DK
    own = srow == scol
    state = st_ref[...]
    inter = []
    for n in range(nchunk):
        r0, r1 = n * c, (n + 1) * c
        inter.append(lax.dot_general(q_dec[r0:r1], state.astype(BF16), (((1,), (1,)), ((), ())),
                                     preferred_element_type=F32))
        kv_t = lax.dot_general(v[r0:r1], k_end[r0:r1], (((0,), (0,)), ((), ())),
                               preferred_element_type=F32)
        state = state * decay[r0:r0 + 1] + jnp.where(own, kv_t, 0.0)
    st_ref[...] = state
    o_inter = jnp.concatenate(inter, axis=0)

    og = og_ref[...].astype(F32)
    gate = og / (1.0 + jnp.exp(-og))
    for h in range(2):
        o = intra[h] + o_inter[:, h * GLA_DV:(h + 1) * GLA_DV]
        ms = jnp.mean(o * o, axis=-1, keepdims=True)
        y = o * lax.rsqrt(ms + EPS) * ng_ref[...] * gate[:, h * GLA_DV:(h + 1) * GLA_DV]
        o_ref[:, h * GLA_DV:(h + 1) * GLA_DV] = y.astype(o_ref.dtype)


def gla_attention(proj, w2p, gate_b, norm_g, *, batch, seq, tb=512):
    tb = min(tb, seq)
    t = batch * seq
    nblk = seq // tb
    masks = _gla_masks(tb)
    kern = functools.partial(_gla_kernel, tb=tb)
    w256 = 2 * GLA_DV
    return pl.pallas_call(
        kern,
        out_shape=jax.ShapeDtypeStruct((t, GLA_HEADS * GLA_DV), BF16),
        grid=(batch, GLA_HEADS // 2, nblk),
        in_specs=[
            pl.BlockSpec((2 * tb, tb), lambda b, h, n: (0, 0)),
            pl.BlockSpec((LANES, LANES), lambda b, h, n: (0, h)),
            pl.BlockSpec((1, LANES), lambda b, h, n: (0, h)),
            pl.BlockSpec((1, GLA_DV), lambda b, h, n: (0, 0)),
            pl.BlockSpec((tb, LANES), lambda b, h, n: (b * nblk + n, C_GQ // LANES + h)),
            pl.BlockSpec((tb, LANES), lambda b, h, n: (b * nblk + n, C_GK // LANES + h)),
            pl.BlockSpec((tb, w256), lambda b, h, n: (b * nblk + n, C_GV // w256 + h)),
            pl.BlockSpec((tb, w256), lambda b, h, n: (b * nblk + n, C_OG // w256 + h)),
            pl.BlockSpec((tb, LANES), lambda b, h, n: (b * nblk + n, C_LR // LANES)),
        ],
        out_specs=pl.BlockSpec((tb, w256), lambda b, h, n: (b * nblk + n, h)),
        scratch_shapes=[pltpu.VMEM((2 * GLA_DV, LANES), F32)],
        compiler_params=_cparams(("parallel", "parallel", "arbitrary")),
        name="gla_attention",
    )(masks, w2p, gate_b.reshape(1, -1).astype(F32), norm_g.reshape(1, -1).astype(F32),
      proj, proj, proj, proj, proj)


def _mix_out_kernel(yd_ref, ys_ref, yg_ref, wd_ref, ws_ref, wg_ref, h_ref, o_ref):
    acc = jnp.dot(yd_ref[...], wd_ref[...], preferred_element_type=F32)
    acc += jnp.dot(ys_ref[...], ws_ref[...], preferred_element_type=F32)
    acc += jnp.dot(yg_ref[...], wg_ref[...], preferred_element_type=F32)
    o_ref[...] = h_ref[...] + acc


def mix_out(yd, ys, yg, wd, ws, wg, h, *, tm):
    m, d = h.shape
    row = lambda i: (i, 0)
    fixed = lambda i: (0, 0)
    return pl.pallas_call(
        _mix_out_kernel,
        out_shape=jax.ShapeDtypeStruct((m, d), F32),
        grid=(m // tm,),
        in_specs=[
            pl.BlockSpec((tm, yd.shape[1]), row),
            pl.BlockSpec((tm, ys.shape[1]), row),
            pl.BlockSpec((tm, yg.shape[1]), row),
            pl.BlockSpec(wd.shape, fixed),
            pl.BlockSpec(ws.shape, fixed),
            pl.BlockSpec(wg.shape, fixed),
            pl.BlockSpec((tm, d), row),
        ],
        out_specs=pl.BlockSpec((tm, d), row),
        compiler_params=_cparams(("parallel",)),
        name="mix_out",
    )(yd, ys, yg, wd, ws, wg, h)


def _xattn_kernel(h_ref, g_ref, wq_ref, kv_ref, wo_ref, gf_ref, o_ref, xn_ref):
    x = h_ref[...]
    ms = jnp.mean(x * x, axis=-1, keepdims=True)
    xn = (x * lax.rsqrt(ms + EPS) * g_ref[...]).astype(BF16)
    q = jnp.dot(xn, wq_ref[...], preferred_element_type=F32).astype(BF16)
    d_xa = XA_HEADS * XA_DH
    outs = []
    for hd in range(XA_HEADS):
        kh = kv_ref[:, hd * XA_DH:(hd + 1) * XA_DH]
        vh = kv_ref[:, d_xa + hd * XA_DH:d_xa + (hd + 1) * XA_DH]
        s = lax.dot_general(q[:, hd * XA_DH:(hd + 1) * XA_DH], kh, (((1,), (1,)), ((), ())),
                            preferred_element_type=F32)
        m = jnp.max(s, axis=-1, keepdims=True)
        p = jnp.exp(s - m)
        l = jnp.sum(p, axis=-1, keepdims=True)
        outs.append((jnp.dot(p.astype(BF16), vh, preferred_element_type=F32) / l).astype(BF16))
    o = jnp.concatenate(outs, axis=1)
    y = x + jnp.dot(o, wo_ref[...], preferred_element_type=F32)
    o_ref[...] = y
    ms2 = jnp.mean(y * y, axis=-1, keepdims=True)
    xn_ref[...] = (y * lax.rsqrt(ms2 + EPS) * gf_ref[...]).astype(BF16)


def cross_attention(h, g, wq, kv, wo, g_ffn, *, batch, seq, n_mem, tm):
    m, d = h.shape
    nt = seq // tm
    fixed = lambda b, i: (0, 0)
    row = lambda b, i: (b * nt + i, 0)
    return pl.pallas_call(
        _xattn_kernel,
        out_shape=(jax.ShapeDtypeStruct((m, d), F32), jax.ShapeDtypeStruct((m, d), BF16)),
        grid=(batch, nt),
        in_specs=[
            pl.BlockSpec((tm, d), row),
            pl.BlockSpec((1, d), fixed),
            pl.BlockSpec(wq.shape, fixed),
            pl.BlockSpec((n_mem, kv.shape[1]), lambda b, i: (b, 0)),
            pl.BlockSpec(wo.shape, fixed),
            pl.BlockSpec((1, d), fixed),
        ],
        out_specs=(pl.BlockSpec((tm, d), row), pl.BlockSpec((tm, d), row)),
        compiler_params=_cparams(("parallel", "arbitrary")),
        name="cross_attention",
    )(h, g.reshape(1, d), wq, kv, wo, g_ffn.reshape(1, d))


def _ffn_up_kernel(x_ref, xp_ref, wg_ref, wu_ref, cw_ref, cb_ref, o_ref, xe_ref, ga_ref, ua_ref, gb_ref, ub_ref,
                   *, tm, tiles_per_seq, nf, d_ff):
    i = pl.program_id(0)
    j = pl.program_id(1)
    hl = CONV_HALO
    ck = FFN_CHUNK

    @pl.when(j == 0)
    def _():
        first = (i % tiles_per_seq) == 0
        prev = xp_ref[...]
        xe_ref[0:hl, :] = jnp.where(first, jnp.zeros_like(prev), prev)
        xe_ref[hl:hl + tm, :] = x_ref[...]
        gb_ref[...] = jnp.zeros_like(gb_ref)
        ub_ref[...] = jnp.zeros_like(ub_ref)

    ts = FFN_STRIP
    nstrip = tm // ts

    def matmuls(c0, g_out, u_out, k):
        r0 = 0 if k == 0 else hl + k * ts
        r1 = hl + (k + 1) * ts
        xs = xe_ref[r0:r1, :]
        g_out[r0:r1, :] = jnp.dot(xs, wg_ref[:, c0:c0 + ck], preferred_element_type=F32)
        u_out[r0:r1, :] = jnp.dot(xs, wu_ref[:, c0:c0 + ck], preferred_element_type=F32)

    def conv(h_ref, col, k):
        hh = h_ref[k * ts:k * ts + hl + ts, :]
        cw = cw_ref[:, pl.ds(col, ck)]
        return (hh[hl - 2:hl - 2 + ts] * cw[0:1] + hh[hl - 1:hl - 1 + ts] * cw[1:2]
                + hh[hl:hl + ts] * cw[2:3] + cb_ref[:, pl.ds(col, ck)])

    def epilogue(g_in, u_in, col, k):
        col = pl.multiple_of(col, ck)
        gate = conv(g_in, col, k)
        up = conv(u_in, pl.multiple_of(d_ff + col, ck), k)
        o_ref[k * ts:(k + 1) * ts, pl.ds(col, ck)] = (gate / (1.0 + jnp.exp(-gate)) * up).astype(o_ref.dtype)

    col_a = j * (2 * ck)
    col_prev = jnp.maximum(col_a - ck, 0)
    for k in range(nstrip):
        matmuls(0, ga_ref, ua_ref, k)
        epilogue(gb_ref, ub_ref, col_prev, k)
    for k in range(nstrip):
        matmuls(ck, gb_ref, ub_ref, k)
        epilogue(ga_ref, ua_ref, col_a, k)

    @pl.when(j == nf - 1)
    def _():
        for k in range(nstrip):
            epilogue(gb_ref, ub_ref, col_a + ck, k)


def ffn_up(xn, w_up, conv_w, conv_b, *, seq, tm):
    m, d = xn.shape
    d_ff = w_up.shape[1] // 2
    tf = 2 * FFN_CHUNK
    nf = d_ff // tf
    hl = CONV_HALO
    kern = functools.partial(_ffn_up_kernel, tm=tm, tiles_per_seq=seq // tm, nf=nf, d_ff=d_ff)
    raw = pltpu.VMEM((tm + hl, FFN_CHUNK), F32)
    return pl.pallas_call(
        kern,
        out_shape=jax.ShapeDtypeStruct((m, d_ff), BF16),
        grid=(m // tm, nf),
        in_specs=[
            pl.BlockSpec((tm, d), lambda i, j: (i, 0)),
            pl.BlockSpec((hl, d), lambda i, j: (jnp.maximum(i * (tm // hl) - 1, 0), 0)),
            pl.BlockSpec((d, tf), lambda i, j: (0, j)),
            pl.BlockSpec((d, tf), lambda i, j: (0, nf + j)),
            pl.BlockSpec((CONV_W, 2 * d_ff), lambda i, j: (0, 0)),
            pl.BlockSpec((1, 2 * d_ff), lambda i, j: (0, 0)),
        ],
        out_specs=pl.BlockSpec((tm, d_ff), lambda i, j: (i, 0)),
        scratch_shapes=[pltpu.VMEM((tm + hl, d), BF16), raw, raw, raw, raw],
        compiler_params=_cparams(("parallel", "arbitrary")),
        name="ffn_up",
    )(xn, xn, w_up, w_up, conv_w, conv_b.reshape(1, -1))


def _matmul_res_kernel(a_ref, w_ref, r_ref, o_ref):
    o_ref[...] = r_ref[...] + jnp.dot(a_ref[...], w_ref[...], preferred_element_type=F32)


def matmul_residual(a, w, res, *, tm, tn):
    m, k = a.shape
    n = w.shape[1]
    return pl.pallas_call(
        _matmul_res_kernel,
        out_shape=jax.ShapeDtypeStruct((m, n), F32),
        grid=(m // tm, n // tn),
        in_specs=[
            pl.BlockSpec((tm, k), lambda i, j: (i, 0)),
            pl.BlockSpec((k, tn), lambda i, j: (0, j)),
            pl.BlockSpec((tm, tn), lambda i, j: (i, j)),
        ],
        out_specs=pl.BlockSpec((tm, tn), lambda i, j: (i, j)),
        compiler_params=_cparams(("parallel", "arbitrary")),
        name="matmul_residual",
    )(a, w, res)


def _rmsnorm_kernel(x_ref, g_ref, o_ref):
    x = x_ref[...]
    ms = jnp.mean(x * x, axis=-1, keepdims=True)
    o_ref[...] = x * lax.rsqrt(ms + EPS) * g_ref[...]


def rmsnorm(x, g, *, tm):
    m, d = x.shape
    return pl.pallas_call(
        _rmsnorm_kernel,
        out_shape=jax.ShapeDtypeStruct((m, d), F32),
        grid=(m // tm,),
        in_specs=[pl.BlockSpec((tm, d), lambda i: (i, 0)), pl.BlockSpec((1, d), lambda i: (0, 0))],
        out_specs=pl.BlockSpec((tm, d), lambda i: (i, 0)),
        compiler_params=_cparams(("parallel",)),
        name="final_rmsnorm",
    )(x, g.reshape(1, d))


def _prep_w_in(w):
    d = w.shape[0]
    scale = jnp.ones((C_OG,), F32)
    scale = scale.at[C_DQ:C_DK].set(DIFF_DQK ** -0.5)
    scale = scale.at[C_SQ:C_SK].set(SWA_DH ** -0.5)
    scale = scale.at[C_GQ:C_GK].set(GLA_DK ** -0.5)
    lr0 = C_OG
    og0 = lr0 + GLA_RANK
    main = w[:, :C_OG] * scale
    og = w[:, og0:og0 + GLA_HEADS * GLA_DV]
    lr = w[:, lr0:lr0 + GLA_RANK]
    pad = jnp.zeros((d, N_PROJ - C_LR - GLA_RANK), w.dtype)
    return jnp.concatenate([main, og, lr, pad], axis=1).astype(BF16)


def kernel(x, mem, norm_mix_g, w_in, diff_lambda, diff_subln_g, swa_sinks, gla_gate_w2, gla_gate_b, gla_norm_g, w_out, norm_xa_g, norm_mem_g, xa_wq, xa_wkv, xa_wo, norm_ffn_g, ffn_w_up, ffn_conv_w, ffn_conv_b, ffn_w_down, final_norm_g):
    batch, seq, d = x.shape
    n_mem = mem.shape[1]
    depth = w_in.shape[0]
    t = batch * seq
    tm = min(512, seq)
    tm_big = min(1024, seq)

    h = x.reshape(t, d)
    memf = mem.reshape(batch * n_mem, d)
    d_diff = DIFF_HEADS * DIFF_DV
    d_swa = SWA_HEADS * SWA_DH

    for l in range(depth):
        lambda_init = 0.8 - 0.6 * math.exp(-0.3 * l)
        linit = jnp.full((1, 1), lambda_init, F32)

        proj = norm_matmul(h, norm_mix_g[l], _prep_w_in(w_in[l]), tm=tm_big, tn=1536)
        y_diff = diff_attention(proj, diff_lambda[l], linit, diff_subln_g[l], batch=batch, seq=seq)
        y_swa = swa_attention(proj, swa_sinks[l], batch=batch, seq=seq)
        w2p = jnp.zeros((LANES, GLA_HEADS * GLA_DK), F32).at[:GLA_RANK].set(gla_gate_w2[l]).astype(BF16)
        y_gla = gla_attention(proj, w2p, gla_gate_b[l], gla_norm_g[l], batch=batch, seq=seq)
        wo = w_out[l].astype(BF16)
        h = mix_out(y_diff, y_swa, y_gla, wo[:d_diff], wo[d_diff:d_diff + d_swa], wo[d_diff + d_swa:], h, tm=tm)

        kv = norm_matmul(memf, norm_mem_g[l], xa_wkv[l].astype(BF16), tm=min(512, batch * n_mem), tn=512)
        wq = (xa_wq[l] * (XA_DH ** -0.5)).astype(BF16)
        h, xn = cross_attention(h, norm_xa_g[l], wq, kv, xa_wo[l].astype(BF16), norm_ffn_g[l],
                                batch=batch, seq=seq, n_mem=n_mem, tm=tm)

        act = ffn_up(xn, ffn_w_up[l].astype(BF16), ffn_conv_w[l], ffn_conv_b[l], seq=seq, tm=tm_big)
        h = matmul_residual(act, ffn_w_down[l].astype(BF16), h, tm=tm_big, tn=512)

    out = rmsnorm(h, final_norm_g, tm=tm)
    return out.reshape(batch, seq, d)
```

```python
import functools
import math

import jax
import jax.numpy as jnp
import numpy as np
from jax import lax
from jax.experimental import pallas as pl
from jax.experimental.pallas import tpu as pltpu

F32 = jnp.float32
BF16 = jnp.bfloat16
EPS = 1e-6
NEG = -1e30
LOG2E = math.log2(math.e)

LANES = 128
VMEM_LIMIT = 56 * 1024 * 1024

DIFF_HEADS = 8
DIFF_DQK = 32
DIFF_DV = 64
SWA_HEADS = 16
SWA_KV_HEADS = 2
SWA_DH = 64
SWA_WINDOW = 128
GLA_HEADS = 4
GLA_DK = 64
GLA_DV = 128
GLA_RANK = 16
GLA_TAU = 16.0
GLA_CHUNK = 64
XA_HEADS = 4
XA_DH = 128
CONV_W = 3
CONV_HALO = 16
FFN_CHUNK = 256
FFN_STRIP = 256

C_DQ, C_DK, C_DV = 0, 512, 1024
C_SQ, C_SK, C_SV = 1536, 2560, 2688
C_GQ, C_GK, C_GV = 2816, 3072, 3328
C_OG, C_LR = 3840, 4352
N_PROJ = 4608


def _cparams(sem):
    return pltpu.CompilerParams(dimension_semantics=sem, vmem_limit_bytes=VMEM_LIMIT)


def _norm_matmul_kernel(x_ref, g_ref, w_ref, o_ref, xn_ref):
    @pl.when(pl.program_id(1) == 0)
    def _():
        x = x_ref[...]
        ms = jnp.mean(x * x, axis=-1, keepdims=True)
        xn_ref[...] = (x * lax.rsqrt(ms + EPS) * g_ref[...]).astype(BF16)

    o_ref[...] = jnp.dot(xn_ref[...], w_ref[...], preferred_element_type=F32).astype(o_ref.dtype)


def norm_matmul(x, g, w, *, tm, tn, out_dtype=BF16):
    m, k = x.shape
    n = w.shape[1]
    return pl.pallas_call(
        _norm_matmul_kernel,
        out_shape=jax.ShapeDtypeStruct((m, n), out_dtype),
        grid=(m // tm, n // tn),
        in_specs=[
            pl.BlockSpec((tm, k), lambda i, j: (i, 0)),
            pl.BlockSpec((1, k), lambda i, j: (0, 0)),
            pl.BlockSpec((k, tn), lambda i, j: (0, j)),
        ],
        out_specs=pl.BlockSpec((tm, tn), lambda i, j: (i, j)),
        scratch_shapes=[pltpu.VMEM((tm, k), BF16)],
        compiler_params=_cparams(("parallel", "arbitrary")),
        name="norm_matmul",
    )(x, g.reshape(1, k), w)


def _bf16_split3(x):
    parts = []
    r = np.float64(x)
    for _ in range(3):
        bits = np.array([r], np.float32).view(np.uint32)
        bits = (bits + np.uint32(0x7FFF) + ((bits >> np.uint32(16)) & np.uint32(1))) & np.uint32(0xFFFF0000)
        a = float(bits.view(np.float32)[0])
        parts.append(a)
        r = r - a
    return parts


LOG2E_PARTS = _bf16_split3(LOG2E)


def _diff_key_features(seq):
    j = np.arange(seq)
    f = np.zeros((seq, LANES), np.float32)
    for t in range(3):
        f[:, 2 * t] = j // LANES
        f[:, 2 * t + 1] = j % LANES
    f[:, 6] = 1.0
    f[:, 7] = 1.0
    return jnp.asarray(f).astype(BF16)


def _diff_kernel(lam_ref, linit_ref, g_ref, kf_ref, q_ref, k_ref, v_ref, o_ref,
                 s0_ref, s1_ref, s2_ref, s3_ref, m_ref, acc_ref, q4_ref, *, tq, tk, nk, nq):
    hp = pl.program_id(1)
    rows = 4 * tq
    lane = lax.broadcasted_iota(jnp.int32, (tq, LANES), 1)
    left = lane < DIFF_DV
    ones = jnp.ones((tk, LANES), BF16)
    bufs = ((s0_ref, s1_ref), (s2_ref, s3_ref))

    lamv = lam_ref[...]
    lam1 = jnp.exp(jnp.sum(lamv[0:1] * lamv[1:2], axis=-1, keepdims=True))
    lam2 = jnp.exp(jnp.sum(lamv[2:3] * lamv[3:4], axis=-1, keepdims=True))
    linit = linit_ref[...]
    lam_full = lam1 - lam2 + linit

    def qk(kj, s_out):
        start = pl.multiple_of(jnp.minimum(kj, nk - 1) * tk, tk)
        ka = jnp.concatenate([k_ref[pl.ds(start, tk), :], kf_ref[pl.ds(start, tk), :]], axis=1)
        s_out[...] = lax.dot_general(q4_ref[...], ka, (((1,), (1,)), ((), ())), preferred_element_type=F32)

    def start_block(qi):
        pos = qi * tq + lax.broadcasted_iota(jnp.int32, (tq, LANES), 0)
        i_hi = (pos // LANES).astype(F32)
        i_lo = (pos % LANES).astype(F32)

        def features(hl):
            head = (2 * hp + hl + 1).astype(F32)
            slope = jnp.exp2(jnp.zeros((tq, LANES), F32) - head * (8.0 / DIFF_HEADS))
            f = jnp.zeros((tq, LANES), F32)
            for t, part in enumerate(LOG2E_PARTS):
                f = jnp.where(lane == 2 * t, slope * (part * LANES), f)
                f = jnp.where(lane == 2 * t + 1, slope * part, f)
            f = jnp.where(lane == 6, -slope * (LOG2E * LANES) * i_hi, f)
            f = jnp.where(lane == 7, -slope * LOG2E * i_lo, f)
            return f.astype(BF16)

        q = q_ref[pl.ds(pl.multiple_of(qi * tq, tq), tq), :]
        zero = jnp.zeros_like(q)
        feats = [features(0), features(1)]
        for c in range(4):
            q4_ref[c * tq:(c + 1) * tq, :] = jnp.concatenate(
                [jnp.where((lane // DIFF_DQK) == c, q, zero), feats[c // 2]], axis=1)
        m_ref[...] = jnp.full_like(m_ref, NEG)
        acc_ref[...] = jnp.zeros_like(acc_ref)
        qk(0, s0_ref)
        qk(1, s1_ref)

    def values(kj):
        start = pl.multiple_of(kj * tk, tk)
        return jnp.concatenate([v_ref[pl.ds(start, tk), :], ones], axis=1)

    def softmax_pv(qi, blocks):
        def causal(s, kj):
            ii = qi * tq + lax.broadcasted_iota(jnp.int32, (rows, tk), 0) % tq
            jj = kj * tk + lax.broadcasted_iota(jnp.int32, (rows, tk), 1)
            return jnp.where(ii >= jj, s, NEG)

        ss = [causal(s_in[...], kj) if masked else s_in[...] for s_in, kj, masked in blocks]
        smax = ss[0] if len(ss) == 1 else jnp.maximum(ss[0], ss[1])
        m_old = m_ref[...]
        m_new = jnp.maximum(m_old, jnp.max(smax, axis=-1, keepdims=True))
        m_rep = jnp.concatenate([m_new] * (tk // LANES), axis=1)
        p = jnp.concatenate([jnp.exp2(s - m_rep).astype(BF16) for s in ss], axis=1)
        va = jnp.concatenate([values(kj) for _, kj, _ in blocks], axis=0)
        alpha = jnp.exp2(m_old - m_new)
        acc_ref[...] = (jnp.concatenate([alpha, alpha], axis=1) * acc_ref[...]
                        + jnp.dot(p, va, preferred_element_type=F32))
        m_ref[...] = m_new

    def finish_block(qi):
        acc = acc_ref[...]
        n = acc[:, 0:LANES] / acc[:, LANES:2 * LANES]
        a0 = n[0:tq] - lam_full * n[tq:2 * tq]
        a1 = n[2 * tq:3 * tq] - lam_full * n[3 * tq:4 * tq]
        o = jnp.where(left, a0, a1)
        sq = o * o
        ms0 = jnp.sum(jnp.where(left, sq, 0.0), axis=-1, keepdims=True) / DIFF_DV
        ms1 = jnp.sum(jnp.where(left, 0.0, sq), axis=-1, keepdims=True) / DIFF_DV
        ms = jnp.where(left, ms0, ms1)
        y = o * lax.rsqrt(ms + EPS) * g_ref[...] * (1.0 - linit)
        o_ref[pl.ds(pl.multiple_of(qi * tq, tq), tq), :] = y.astype(o_ref.dtype)

    def query_block(qi, carry):
        nfull = lax.div(qi * tq, tk)
        npair = lax.div(nfull, 2)

        def step(t, cur, nxt):
            qk(2 * t + 2, nxt[0])
            qk(2 * t + 3, nxt[1])
            softmax_pv(qi, [(cur[0], 2 * t, False), (cur[1], 2 * t + 1, False)])

        def pair(t, c):
            lax.cond(lax.rem(t, 2) == 0, lambda: step(t, bufs[0], bufs[1]), lambda: step(t, bufs[1], bufs[0]))
            return c

        lax.fori_loop(0, npair, pair, 0)

        def tail(cur):
            has_full = lax.rem(nfull, 2) == 1

            @pl.when(has_full)
            def _():
                softmax_pv(qi, [(cur[0], nfull - 1, False), (cur[1], nfull, True)])

            @pl.when(jnp.logical_not(has_full))
            def _():
                softmax_pv(qi, [(cur[0], nfull, True)])

        lax.cond(lax.rem(npair, 2) == 0, lambda: tail(bufs[0]), lambda: tail(bufs[1]))
        finish_block(qi)
        start_block(jnp.minimum(qi + 1, nq - 1))
        return carry

    start_block(0)
    lax.fori_loop(0, nq, query_block, 0)


def diff_attention(proj, lam, linit, subln_g, *, batch, seq, tq=256, tk=256):
    t = batch * seq
    g2 = jnp.concatenate([subln_g, subln_g]).reshape(1, LANES).astype(F32)
    kern = functools.partial(_diff_kernel, tq=tq, tk=tk, nk=seq // tk, nq=seq // tq)
    fixed = lambda b, h: (0, 0)
    sbuf = pltpu.VMEM((4 * tq, tk), F32)
    return pl.pallas_call(
        kern,
        out_shape=jax.ShapeDtypeStruct((t, DIFF_HEADS * DIFF_DV), BF16),
        grid=(batch, DIFF_HEADS // 2),
        in_specs=[
            pl.BlockSpec((4, DIFF_DQK), fixed),
            pl.BlockSpec((1, 1), fixed),
            pl.BlockSpec((1, LANES), fixed),
            pl.BlockSpec((seq, LANES), fixed),
            pl.BlockSpec((seq, LANES), lambda b, h: (b, C_DQ // LANES + h)),
            pl.BlockSpec((seq, LANES), lambda b, h: (b, C_DK // LANES + h)),
            pl.BlockSpec((seq, LANES), lambda b, h: (b, C_DV // LANES + h)),
        ],
        out_specs=pl.BlockSpec((seq, LANES), lambda b, h: (b, h)),
        scratch_shapes=[sbuf, sbuf, sbuf, sbuf,
                        pltpu.VMEM((4 * tq, LANES), F32),
                        pltpu.VMEM((4 * tq, 2 * LANES), F32),
                        pltpu.VMEM((4 * tq, 2 * LANES), BF16)],
        compiler_params=_cparams(("parallel", "parallel")),
        name="diff_attention",
    )(lam.astype(F32), linit, g2, _diff_key_features(seq), proj, proj, proj)


def _swa_bias_tables():
    w = SWA_WINDOW
    rper = SWA_HEADS // SWA_KV_HEADS
    i = np.arange(w)[:, None]
    j = np.arange(2 * w)[None, :]
    dist = (i + w - j).astype(np.float64)
    valid = (dist >= 0) & (dist < w)
    slopes = 2.0 ** (-8.0 * np.arange(1, SWA_HEADS + 1, dtype=np.float64) / SWA_HEADS)
    out = np.empty((2, SWA_KV_HEADS, rper * w, 2 * w), np.float32)
    for variant, ok in enumerate((valid & (j >= w), valid)):
        for g in range(SWA_KV_HEADS):
            for r in range(rper):
                out[variant, g, r * w:(r + 1) * w] = np.where(ok, slopes[g * rper + r] * dist, 1e30)
    return jnp.asarray(out)


def _swa_kernel(b0_ref, b1_ref, sk_ref, q_ref, kp_ref, kc_ref, vp_ref, vc_ref, o_ref):
    g = pl.program_id(1)
    w = SWA_WINDOW
    rper = SWA_HEADS // SWA_KV_HEADS
    lane = lax.broadcasted_iota(jnp.int32, (3 * w, LANES), 1)
    mine = (lane // SWA_DH) == g

    def dup(prev_ref, cur_ref):
        x = jnp.concatenate([prev_ref[...], cur_ref[...]], axis=0).astype(F32)
        return jnp.where(mine, x, pltpu.roll(x, SWA_DH, axis=1)).astype(BF16)

    kk = dup(kp_ref, kc_ref)
    vv = jnp.concatenate([dup(vp_ref, vc_ref), jnp.ones((3 * w, LANES), BF16)], axis=1)
    qlane = lax.broadcasted_iota(jnp.int32, (w, LANES), 1)
    left = qlane < SWA_DH
    sink = sk_ref[0]

    for blk, b_ref in enumerate((b0_ref, b1_ref)):
        r0 = blk * w
        parts = []
        for p in range(rper // 2):
            q2 = q_ref[r0:r0 + w, p * LANES:(p + 1) * LANES]
            zero = jnp.zeros_like(q2)
            parts += [jnp.where(left, q2, zero), jnp.where(left, zero, q2)]
        qm = jnp.concatenate(parts, axis=0)
        s = lax.dot_general(qm, kk[r0:r0 + 2 * w], (((1,), (1,)), ((), ())), preferred_element_type=F32)
        u = s - b_ref[0, 0]
        m = jnp.maximum(jnp.max(u, axis=-1, keepdims=True), sink)
        pr = jnp.exp(u - jnp.concatenate([m, m], axis=1))
        acc = jnp.dot(pr.astype(BF16), vv[r0:r0 + 2 * w], preferred_element_type=F32)
        l = acc[:, LANES:2 * LANES] + jnp.exp(sink - m)
        o = acc[:, 0:LANES] / l
        for p in range(rper // 2):
            pair = jnp.where(left, o[(2 * p) * w:(2 * p + 1) * w], o[(2 * p + 1) * w:(2 * p + 2) * w])
            o_ref[r0:r0 + w, p * LANES:(p + 1) * LANES] = pair.astype(o_ref.dtype)


def swa_attention(proj, sinks, *, batch, seq):
    t = batch * seq
    w = SWA_WINDOW
    nb2 = seq // (2 * w)
    rper = SWA_HEADS // SWA_KV_HEADS
    bias = _swa_bias_tables()
    sink_rows = jnp.broadcast_to(sinks.astype(F32).reshape(SWA_KV_HEADS, rper, 1, 1),
                                 (SWA_KV_HEADS, rper, w, LANES)).reshape(SWA_KV_HEADS, rper * w, LANES)
    qw = rper * SWA_DH
    kcol = C_SK // LANES
    vcol = C_SV // LANES

    def prev(b, n):
        return 2 * (b * nb2 + n) - jnp.where(n > 0, 1, 0)

    return pl.pallas_call(
        _swa_kernel,
        out_shape=jax.ShapeDtypeStruct((t, SWA_HEADS * SWA_DH), BF16),
        grid=(batch, SWA_KV_HEADS, nb2),
        in_specs=[
            pl.BlockSpec((1, 1, rper * w, 2 * w), lambda b, g, n: (jnp.minimum(n, 1), g, 0, 0)),
            pl.BlockSpec((1, 1, rper * w, 2 * w), lambda b, g, n: (1, g, 0, 0)),
            pl.BlockSpec((1, rper * w, LANES), lambda b, g, n: (g, 0, 0)),
            pl.BlockSpec((2 * w, qw), lambda b, g, n: (b * nb2 + n, C_SQ // qw + g)),
            pl.BlockSpec((w, LANES), lambda b, g, n: (prev(b, n), kcol)),
            pl.BlockSpec((2 * w, LANES), lambda b, g, n: (b * nb2 + n, kcol)),
            pl.BlockSpec((w, LANES), lambda b, g, n: (prev(b, n), vcol)),
            pl.BlockSpec((2 * w, LANES), lambda b, g, n: (b * nb2 + n, vcol)),
        ],
        out_specs=pl.BlockSpec((2 * w, qw), lambda b, g, n: (b * nb2 + n, g)),
        compiler_params=_cparams(("parallel", "parallel", "arbitrary")),
        name="swa_attention",
    )(bias, bias, sink_rows, proj, proj, proj, proj, proj)


def _gla_masks(tb):
    c = GLA_CHUNK
    t = np.arange(tb)[:, None]
    s = np.arange(tb)[None, :]
    same = (t // c) == (s // c)
    tri = same & (s <= t)
    return jnp.asarray(np.concatenate([tri, same], axis=0).astype(np.float32)).astype(BF16)


def _gla_kernel(mask_ref, w2_ref, gb_ref, ng_ref, q_ref, k_ref, v_ref, og_ref, lr_ref, o_ref, st_ref, *, tb):
    c = GLA_CHUNK
    nchunk = tb // c

    @pl.when(pl.program_id(2) == 0)
    def _():
        st_ref[...] = jnp.zeros_like(st_ref)

    z = jnp.dot(lr_ref[...], w2_ref[...], preferred_element_type=F32) + gb_ref[...]
    log_a = (jnp.minimum(z, 0.0) - jnp.log1p(jnp.exp(-jnp.abs(z)))) / GLA_TAU

    hi = log_a.astype(BF16)
    lo = (log_a - hi.astype(F32)).astype(BF16)
    hl = jnp.concatenate([hi, lo], axis=1)
    cs = jnp.dot(mask_ref[...], hl, preferred_element_type=F32)
    b = cs[0:tb, 0:LANES] + cs[0:tb, LANES:2 * LANES]
    b_last = cs[tb:2 * tb, 0:LANES] + cs[tb:2 * tb, LANES:2 * LANES]

    qf = q_ref[...].astype(F32)
    kf = k_ref[...].astype(F32)
    q_dec = (qf * jnp.exp(b)).astype(BF16)
    k_inv = (kf * jnp.exp(-b)).astype(BF16)
    k_end = (kf * jnp.exp(b_last - b)).astype(BF16)
    decay = jnp.exp(b_last)

    lane = lax.broadcasted_iota(jnp.int32, (tb, LANES), 1)
    left = lane < GLA_DK
    zero = jnp.zeros_like(q_dec)
    qd = [jnp.where(left, q_dec, zero), jnp.where(left, zero, q_dec)]
    tri = mask_ref[0:tb, :] > 0
    v = v_ref[...]

    intra = []
    for h in range(2):
        a = lax.dot_general(qd[h], k_inv, (((1,), (1,)), ((), ())), preferred_element_type=F32)
        a = jnp.where(tri, a, 0.0).astype(BF16)
        intra.append(jnp.dot(a, v[:, h * GLA_DV:(h + 1) * GLA_DV], preferred_element_type=F32))

    srow = lax.broadcasted_iota(jnp.int32, (2 * GLA_DV, LANES), 0) // GLA_DV
    scol = lax.broadcasted_iota(jnp.int32, (2 * GLA_DV, LANES), 1) // GLA_DK
    own = srow == scol
    state = st_ref[...]
    inter = []
    for n in range(nchunk):
        r0, r1 = n * c, (n + 1) * c
        inter.append(lax.dot_general(q_dec[r0:r1], state.astype(BF16), (((1,), (1,)), ((), ())),
                                     preferred_element_type=F32))
        kv_t = lax.dot_general(v[r0:r1], k_end[r0:r1], (((0,), (0,)), ((), ())),
                               preferred_element_type=F32)
        state = state * decay[r0:r0 + 1] + jnp.where(own, kv_t, 0.0)
    st_ref[...] = state
    o_inter = jnp.concatenate(inter, axis=0)

    og = og_ref[...].astype(F32)
    gate = og / (1.0 + jnp.exp(-og))
    for h in range(2):
        o = intra[h] + o_inter[:, h * GLA_DV:(h + 1) * GLA_DV]
        ms = jnp.mean(o * o, axis=-1, keepdims=True)
        y = o * lax.rsqrt(ms + EPS) * ng_ref[...] * gate[:, h * GLA_DV:(h + 1) * GLA_DV]
        o_ref[:, h * GLA_DV:(h + 1) * GLA_DV] = y.astype(o_ref.dtype)


def gla_attention(proj, w2p, gate_b, norm_g, *, batch, seq, tb=512):
    tb = min(tb, seq)
    t = batch * seq
    nblk = seq // tb
    masks = _gla_masks(tb)
    kern = functools.partial(_gla_kernel, tb=tb)
    w256 = 2 * GLA_DV
    return pl.pallas_call(
        kern,
        out_shape=jax.ShapeDtypeStruct((t, GLA_HEADS * GLA_DV), BF16),
        grid=(batch, GLA_HEADS // 2, nblk),
        in_specs=[
            pl.BlockSpec((2 * tb, tb), lambda b, h, n: (0, 0)),
            pl.BlockSpec((LANES, LANES), lambda b, h, n: (0, h)),
            pl.BlockSpec((1, LANES), lambda b, h, n: (0, h)),
            pl.BlockSpec((1, GLA_DV), lambda b, h, n: (0, 0)),
            pl.BlockSpec((tb, LANES), lambda b, h, n: (b * nblk + n, C_GQ // LANES + h)),
            pl.BlockSpec((tb, LANES), lambda b, h, n: (b * nblk + n, C_GK // LANES + h)),
            pl.BlockSpec((tb, w256), lambda b, h, n: (b * nblk + n, C_GV // w256 + h)),
            pl.BlockSpec((tb, w256), lambda b, h, n: (b * nblk + n, C_OG // w256 + h)),
            pl.BlockSpec((tb, LANES), lambda b, h, n: (b * nblk + n, C_LR // LANES)),
        ],
        out_specs=pl.BlockSpec((tb, w256), lambda b, h, n: (b * nblk + n, h)),
        scratch_shapes=[pltpu.VMEM((2 * GLA_DV, LANES), F32)],
        compiler_params=_cparams(("parallel", "parallel", "arbitrary")),
        name="gla_attention",
    )(masks, w2p, gate_b.reshape(1, -1).astype(F32), norm_g.reshape(1, -1).astype(F32),
      proj, proj, proj, proj, proj)


def _mix_out_kernel(yd_ref, ys_ref, yg_ref, wd_ref, ws_ref, wg_ref, h_ref, o_ref):
    acc = jnp.dot(yd_ref[...], wd_ref[...], preferred_element_type=F32)
    acc += jnp.dot(ys_ref[...], ws_ref[...], preferred_element_type=F32)
    acc += jnp.dot(yg_ref[...], wg_ref[...], preferred_element_type=F32)
    o_ref[...] = h_ref[...] + acc


def mix_out(yd, ys, yg, wd, ws, wg, h, *, tm):
    m, d = h.shape
    row = lambda i: (i, 0)
    fixed = lambda i: (0, 0)
    return pl.pallas_call(
        _mix_out_kernel,
        out_shape=jax.ShapeDtypeStruct((m, d), F32),
        grid=(m // tm,),
        in_specs=[
            pl.BlockSpec((tm, yd.shape[1]), row),
            pl.BlockSpec((tm, ys.shape[1]), row),
            pl.BlockSpec((tm, yg.shape[1]), row),
            pl.BlockSpec(wd.shape, fixed),
            pl.BlockSpec(ws.shape, fixed),
            pl.BlockSpec(wg.shape, fixed),
            pl.BlockSpec((tm, d), row),
        ],
        out_specs=pl.BlockSpec((tm, d), row),
        compiler_params=_cparams(("parallel",)),
        name="mix_out",
    )(yd, ys, yg, wd, ws, wg, h)


def _xattn_kernel(h_ref, g_ref, wq_ref, kv_ref, wo_ref, gf_ref, o_ref, xn_ref):
    x = h_ref[...]
    ms = jnp.mean(x * x, axis=-1, keepdims=True)
    xn = (x * lax.rsqrt(ms + EPS) * g_ref[...]).astype(BF16)
    q = jnp.dot(xn, wq_ref[...], preferred_element_type=F32).astype(BF16)
    d_xa = XA_HEADS * XA_DH
    outs = []
    for hd in range(XA_HEADS):
        kh = kv_ref[:, hd * XA_DH:(hd + 1) * XA_DH]
        vh = kv_ref[:, d_xa + hd * XA_DH:d_xa + (hd + 1) * XA_DH]
        s = lax.dot_general(q[:, hd * XA_DH:(hd + 1) * XA_DH], kh, (((1,), (1,)), ((), ())),
                            preferred_element_type=F32)
        m = jnp.max(s, axis=-1, keepdims=True)
        p = jnp.exp(s - m)
        l = jnp.sum(p, axis=-1, keepdims=True)
        outs.append((jnp.dot(p.astype(BF16), vh, preferred_element_type=F32) / l).astype(BF16))
    o = jnp.concatenate(outs, axis=1)
    y = x + jnp.dot(o, wo_ref[...], preferred_element_type=F32)
    o_ref[...] = y
    ms2 = jnp.mean(y * y, axis=-1, keepdims=True)
    xn_ref[...] = (y * lax.rsqrt(ms2 + EPS) * gf_ref[...]).astype(BF16)


def cross_attention(h, g, wq, kv, wo, g_ffn, *, batch, seq, n_mem, tm):
    m, d = h.shape
    nt = seq // tm
    fixed = lambda b, i: (0, 0)
    row = lambda b, i: (b * nt + i, 0)
    return pl.pallas_call(
        _xattn_kernel,
        out_shape=(jax.ShapeDtypeStruct((m, d), F32), jax.ShapeDtypeStruct((m, d), BF16)),
        grid=(batch, nt),
        in_specs=[
            pl.BlockSpec((tm, d), row),
            pl.BlockSpec((1, d), fixed),
            pl.BlockSpec(wq.shape, fixed),
            pl.BlockSpec((n_mem, kv.shape[1]), lambda b, i: (b, 0)),
            pl.BlockSpec(wo.shape, fixed),
            pl.BlockSpec((1, d), fixed),
        ],
        out_specs=(pl.BlockSpec((tm, d), row), pl.BlockSpec((tm, d), row)),
        compiler_params=_cparams(("parallel", "arbitrary")),
        name="cross_attention",
    )(h, g.reshape(1, d), wq, kv, wo, g_ffn.reshape(1, d))


def _ffn_up_kernel(x_ref, xp_ref, wg_ref, wu_ref, cw_ref, cb_ref, o_ref, xe_ref, ga_ref, ua_ref, gb_ref, ub_ref,
                   *, tm, tiles_per_seq, nf, d_ff):
    i = pl.program_id(0)
    j = pl.program_id(1)
    hl = CONV_HALO
    ck = FFN_CHUNK

    @pl.when(j == 0)
    def _():
        first = (i % tiles_per_seq) == 0
        prev = xp_ref[...]
        xe_ref[0:hl, :] = jnp.where(first, jnp.zeros_like(prev), prev)
        xe_ref[hl:hl + tm, :] = x_ref[...]
        gb_ref[...] = jnp.zeros_like(gb_ref)
        ub_ref[...] = jnp.zeros_like(ub_ref)

    ts = FFN_STRIP
    nstrip = tm // ts

    def matmuls(c0, g_out, u_out, k):
        r0 = 0 if k == 0 else hl + k * ts
        r1 = hl + (k + 1) * ts
        xs = xe_ref[r0:r1, :]
        g_out[r0:r1, :] = jnp.dot(xs, wg_ref[:, c0:c0 + ck], preferred_element_type=F32)
        u_out[r0:r1, :] = jnp.dot(xs, wu_ref[:, c0:c0 + ck], preferred_element_type=F32)

    def conv(h_ref, col, k):
        hh = h_ref[k * ts:k * ts + hl + ts, :]
        cw = cw_ref[:, pl.ds(col, ck)]
        return (hh[hl - 2:hl - 2 + ts] * cw[0:1] + hh[hl - 1:hl - 1 + ts] * cw[1:2]
                + hh[hl:hl + ts] * cw[2:3] + cb_ref[:, pl.ds(col, ck)])

    def epilogue(g_in, u_in, col, k):
        col = pl.multiple_of(col, ck)
        gate = conv(g_in, col, k)
        up = conv(u_in, pl.multiple_of(d_ff + col, ck), k)
        o_ref[k * ts:(k + 1) * ts, pl.ds(col, ck)] = (gate / (1.0 + jnp.exp(-gate)) * up).astype(o_ref.dtype)

    col_a = j * (2 * ck)
    col_prev = jnp.maximum(col_a - ck, 0)
    for k in range(nstrip):
        matmuls(0, ga_ref, ua_ref, k)
        epilogue(gb_ref, ub_ref, col_prev, k)
    for k in range(nstrip):
        matmuls(ck, gb_ref, ub_ref, k)
        epilogue(ga_ref, ua_ref, col_a, k)

    @pl.when(j == nf - 1)
    def _():
        for k in range(nstrip):
            epilogue(gb_ref, ub_ref, col_a + ck, k)


def ffn_up(xn, w_up, conv_w, conv_b, *, seq, tm):
    m, d = xn.shape
    d_ff = w_up.shape[1] // 2
    tf = 2 * FFN_CHUNK
    nf = d_ff // tf
    hl = CONV_HALO
    kern = functools.partial(_ffn_up_kernel, tm=tm, tiles_per_seq=seq // tm, nf=nf, d_ff=d_ff)
    raw = pltpu.VMEM((tm + hl, FFN_CHUNK), F32)
    return pl.pallas_call(
        kern,
        out_shape=jax.ShapeDtypeStruct((m, d_ff), BF16),
        grid=(m // tm, nf),
        in_specs=[
            pl.BlockSpec((tm, d), lambda i, j: (i, 0)),
            pl.BlockSpec((hl, d), lambda i, j: (jnp.maximum(i * (tm // hl) - 1, 0), 0)),
            pl.BlockSpec((d, tf), lambda i, j: (0, j)),
            pl.BlockSpec((d, tf), lambda i, j: (0, nf + j)),
            pl.BlockSpec((CONV_W, 2 * d_ff), lambda i, j: (0, 0)),
            pl.BlockSpec((1, 2 * d_ff), lambda i, j: (0, 0)),
        ],
        out_specs=pl.BlockSpec((tm, d_ff), lambda i, j: (i, 0)),
        scratch_shapes=[pltpu.VMEM((tm + hl, d), BF16), raw, raw, raw, raw],
        compiler_params=_cparams(("parallel", "arbitrary")),
        name="ffn_up",
    )(xn, xn, w_up, w_up, conv_w, conv_b.reshape(1, -1))


def _matmul_res_kernel(a_ref, w_ref, r_ref, o_ref):
    o_ref[...] = r_ref[...] + jnp.dot(a_ref[...], w_ref[...], preferred_element_type=F32)


def matmul_residual(a, w, res, *, tm, tn):
    m, k = a.shape
    n = w.shape[1]
    return pl.pallas_call(
        _matmul_res_kernel,
        out_shape=jax.ShapeDtypeStruct((m, n), F32),
        grid=(m // tm, n // tn),
        in_specs=[
            pl.BlockSpec((tm, k), lambda i, j: (i, 0)),
            pl.BlockSpec((k, tn), lambda i, j: (0, j)),
            pl.BlockSpec((tm, tn), lambda i, j: (i, j)),
        ],
        out_specs=pl.BlockSpec((tm, tn), lambda i, j: (i, j)),
        compiler_params=_cparams(("parallel", "arbitrary")),
        name="matmul_residual",
    )(a, w, res)


def _rmsnorm_kernel(x_ref, g_ref, o_ref):
    x = x_ref[...]
    ms = jnp.mean(x * x, axis=-1, keepdims=True)
    o_ref[...] = x * lax.rsqrt(ms + EPS) * g_ref[...]


def rmsnorm(x, g, *, tm):
    m, d = x.shape
    return pl.pallas_call(
        _rmsnorm_kernel,
        out_shape=jax.ShapeDtypeStruct((m, d), F32),
        grid=(m // tm,),
        in_specs=[pl.BlockSpec((tm, d), lambda i: (i, 0)), pl.BlockSpec((1, d), lambda i: (0, 0))],
        out_specs=pl.BlockSpec((tm, d), lambda i: (i, 0)),
        compiler_params=_cparams(("parallel",)),
        name="final_rmsnorm",
    )(x, g.reshape(1, d))


def _prep_w_in(w):
    d = w.shape[0]
    scale = jnp.ones((C_OG,), F32)
    scale = scale.at[C_DQ:C_DK].set(DIFF_DQK ** -0.5 * LOG2E)
    scale = scale.at[C_SQ:C_SK].set(SWA_DH ** -0.5)
    scale = scale.at[C_GQ:C_GK].set(GLA_DK ** -0.5)
    lr0 = C_OG
    og0 = lr0 + GLA_RANK
    main = w[:, :C_OG] * scale
    og = w[:, og0:og0 + GLA_HEADS * GLA_DV]
    lr = w[:, lr0:lr0 + GLA_RANK]
    pad = jnp.zeros((d, N_PROJ - C_LR - GLA_RANK), w.dtype)
    return jnp.concatenate([main, og, lr, pad], axis=1).astype(BF16)


def kernel(x, mem, norm_mix_g, w_in, diff_lambda, diff_subln_g, swa_sinks, gla_gate_w2, gla_gate_b, gla_norm_g, w_out, norm_xa_g, norm_mem_g, xa_wq, xa_wkv, xa_wo, norm_ffn_g, ffn_w_up, ffn_conv_w, ffn_conv_b, ffn_w_down, final_norm_g):
    batch, seq, d = x.shape
    n_mem = mem.shape[1]
    depth = w_in.shape[0]
    t = batch * seq
    tm = min(512, seq)
    tm_big = min(1024, seq)

    h = x.reshape(t, d)
    memf = mem.reshape(batch * n_mem, d)
    d_diff = DIFF_HEADS * DIFF_DV
    d_swa = SWA_HEADS * SWA_DH

    for l in range(depth):
        lambda_init = 0.8 - 0.6 * math.exp(-0.3 * l)
        linit = jnp.full((1, 1), lambda_init, F32)

        proj = norm_matmul(h, norm_mix_g[l], _prep_w_in(w_in[l]), tm=tm_big, tn=1536)
        y_diff = diff_attention(proj, diff_lambda[l], linit, diff_subln_g[l], batch=batch, seq=seq)
        y_swa = swa_attention(proj, swa_sinks[l], batch=batch, seq=seq)
        w2p = jnp.zeros((LANES, GLA_HEADS * GLA_DK), F32).at[:GLA_RANK].set(gla_gate_w2[l]).astype(BF16)
        y_gla = gla_attention(proj, w2p, gla_gate_b[l], gla_norm_g[l], batch=batch, seq=seq)
        wo = w_out[l].astype(BF16)
        h = mix_out(y_diff, y_swa, y_gla, wo[:d_diff], wo[d_diff:d_diff + d_swa], wo[d_diff + d_swa:], h, tm=tm)

        kv = norm_matmul(memf, norm_mem_g[l], xa_wkv[l].astype(BF16), tm=min(512, batch * n_mem), tn=512)
        wq = (xa_wq[l] * (XA_DH ** -0.5)).astype(BF16)
        h, xn = cross_attention(h, norm_xa_g[l], wq, kv, xa_wo[l].astype(BF16), norm_ffn_g[l],
                                batch=batch, seq=seq, n_mem=n_mem, tm=tm)

        act = ffn_up(xn, ffn_w_up[l].astype(BF16), ffn_conv_w[l], ffn_conv_b[l], seq=seq, tm=tm_big)
        h = matmul_residual(act, ffn_w_down[l].astype(BF16), h, tm=tm_big, tn=512)

    out = rmsnorm(h, final_norm_g, tm=tm)
    return out.reshape(batch, seq, d)
```

```python
import functools
import math

import jax
import jax.numpy as jnp
import numpy as np
from jax import lax
from jax.experimental import pallas as pl
from jax.experimental.pallas import tpu as pltpu

F32 = jnp.float32
BF16 = jnp.bfloat16
EPS = 1e-6
NEG = -1e30
LOG2E = math.log2(math.e)

LANES = 128
VMEM_LIMIT = 56 * 1024 * 1024

DIFF_HEADS = 8
DIFF_DQK = 32
DIFF_DV = 64
SWA_HEADS = 16
SWA_KV_HEADS = 2
SWA_DH = 64
SWA_WINDOW = 128
GLA_HEADS = 4
GLA_DK = 64
GLA_DV = 128
GLA_RANK = 16
GLA_TAU = 16.0
GLA_CHUNK = 64
XA_HEADS = 4
XA_DH = 128
CONV_W = 3
FFN_GROUP = 32
FFN_CHUNK = 256
FFN_STRIP = 256

C_DQ, C_DK, C_DV = 0, 512, 1024
C_SQ, C_SK, C_SV = 1536, 2560, 2688
C_GQ, C_GK, C_GV = 2816, 3072, 3328
C_OG, C_LR = 3840, 4352
N_PROJ = 4608


def _cparams(sem):
    return pltpu.CompilerParams(dimension_semantics=sem, vmem_limit_bytes=VMEM_LIMIT)


def _norm_matmul_kernel(x_ref, g_ref, w_ref, o_ref, xn_ref):
    @pl.when(pl.program_id(1) == 0)
    def _():
        x = x_ref[...]
        ms = jnp.mean(x * x, axis=-1, keepdims=True)
        xn_ref[...] = (x * lax.rsqrt(ms + EPS) * g_ref[...]).astype(BF16)

    o_ref[...] = jnp.dot(xn_ref[...], w_ref[...], preferred_element_type=F32).astype(o_ref.dtype)


def norm_matmul(x, g, w, layer, *, tm, tn, out_dtype=BF16):
    m, k = x.shape
    n = w.shape[2]
    return pl.pallas_call(
        _norm_matmul_kernel,
        out_shape=jax.ShapeDtypeStruct((m, n), out_dtype),
        grid=(m // tm, n // tn),
        in_specs=[
            pl.BlockSpec((tm, k), lambda i, j: (i, 0)),
            pl.BlockSpec((1, k), lambda i, j: (0, 0)),
            pl.BlockSpec((None, k, tn), lambda i, j: (layer, 0, j)),
        ],
        out_specs=pl.BlockSpec((tm, tn), lambda i, j: (i, j)),
        scratch_shapes=[pltpu.VMEM((tm, k), BF16)],
        compiler_params=_cparams(("parallel", "arbitrary")),
        name="norm_matmul",
    )(x, g.reshape(1, k), w)


def _bf16_split3(x):
    parts = []
    r = np.float64(x)
    for _ in range(3):
        bits = np.array([r], np.float32).view(np.uint32)
        bits = (bits + np.uint32(0x7FFF) + ((bits >> np.uint32(16)) & np.uint32(1))) & np.uint32(0xFFFF0000)
        a = float(bits.view(np.float32)[0])
        parts.append(a)
        r = r - a
    return parts


LOG2E_PARTS = _bf16_split3(LOG2E)


def _diff_key_features(seq):
    j = np.arange(seq)
    f = np.zeros((seq, LANES), np.float32)
    for t in range(3):
        f[:, 2 * t] = j // LANES
        f[:, 2 * t + 1] = j % LANES
    f[:, 6] = 1.0
    f[:, 7] = 1.0
    return jnp.asarray(f).astype(BF16)


def _diff_kernel(lam_ref, linit_ref, g_ref, kf_ref, q_ref, k_ref, v_ref, o_ref,
                 s0_ref, s1_ref, s2_ref, s3_ref, m_ref, acc_ref, q4_ref, *, tq, tk, nk, nq):
    hp = pl.program_id(1)
    rows = 4 * tq
    lane = lax.broadcasted_iota(jnp.int32, (tq, LANES), 1)
    left = lane < DIFF_DV
    ones = jnp.ones((tk, LANES), BF16)
    bufs = ((s0_ref, s1_ref), (s2_ref, s3_ref))

    lamv = lam_ref[...]
    lam1 = jnp.exp(jnp.sum(lamv[0:1] * lamv[1:2], axis=-1, keepdims=True))
    lam2 = jnp.exp(jnp.sum(lamv[2:3] * lamv[3:4], axis=-1, keepdims=True))
    linit = linit_ref[...]
    lam_full = lam1 - lam2 + linit

    def qk(kj, s_out):
        start = pl.multiple_of(jnp.minimum(kj, nk - 1) * tk, tk)
        ka = jnp.concatenate([k_ref[pl.ds(start, tk), :], kf_ref[pl.ds(start, tk), :]], axis=1)
        s_out[...] = lax.dot_general(q4_ref[...], ka, (((1,), (1,)), ((), ())), preferred_element_type=F32)

    def start_block(qi):
        pos = qi * tq + lax.broadcasted_iota(jnp.int32, (tq, LANES), 0)
        i_hi = (pos // LANES).astype(F32)
        i_lo = (pos % LANES).astype(F32)

        def features(hl):
            head = (2 * hp + hl + 1).astype(F32)
            slope = jnp.exp2(jnp.zeros((tq, LANES), F32) - head * (8.0 / DIFF_HEADS))
            f = jnp.zeros((tq, LANES), F32)
            for t, part in enumerate(LOG2E_PARTS):
                f = jnp.where(lane == 2 * t, slope * (part * LANES), f)
                f = jnp.where(lane == 2 * t + 1, slope * part, f)
            f = jnp.where(lane == 6, -slope * (LOG2E * LANES) * i_hi, f)
            f = jnp.where(lane == 7, -slope * LOG2E * i_lo, f)
            return f.astype(BF16)

        q = q_ref[pl.ds(pl.multiple_of(qi * tq, tq), tq), :]
        zero = jnp.zeros_like(q)
        feats = [features(0), features(1)]
        for c in range(4):
            q4_ref[c * tq:(c + 1) * tq, :] = jnp.concatenate(
                [jnp.where((lane // DIFF_DQK) == c, q, zero), feats[c // 2]], axis=1)
        m_ref[...] = jnp.full_like(m_ref, NEG)
        acc_ref[...] = jnp.zeros_like(acc_ref)
        qk(0, s0_ref)
        qk(1, s1_ref)

    def values(kj):
        start = pl.multiple_of(kj * tk, tk)
        return jnp.concatenate([v_ref[pl.ds(start, tk), :], ones], axis=1)

    def softmax_pv(qi, blocks):
        def causal(s, kj):
            ii = qi * tq + lax.broadcasted_iota(jnp.int32, (rows, tk), 0) % tq
            jj = kj * tk + lax.broadcasted_iota(jnp.int32, (rows, tk), 1)
            return jnp.where(ii >= jj, s, NEG)

        ss = [causal(s_in[...], kj) if masked else s_in[...] for s_in, kj, masked in blocks]
        smax = ss[0] if len(ss) == 1 else jnp.maximum(ss[0], ss[1])
        m_old = m_ref[...]
        m_new = jnp.maximum(m_old, jnp.max(smax, axis=-1, keepdims=True))
        m_rep = jnp.concatenate([m_new] * (tk // LANES), axis=1)
        p = jnp.concatenate([jnp.exp2(s - m_rep).astype(BF16) for s in ss], axis=1)
        va = jnp.concatenate([values(kj) for _, kj, _ in blocks], axis=0)
        alpha = jnp.exp2(m_old - m_new)
        acc_ref[...] = (jnp.concatenate([alpha, alpha], axis=1) * acc_ref[...]
                        + jnp.dot(p, va, preferred_element_type=F32))
        m_ref[...] = m_new

    def finish_block(qi):
        acc = acc_ref[...]
        n = acc[:, 0:LANES] / acc[:, LANES:2 * LANES]
        a0 = n[0:tq] - lam_full * n[tq:2 * tq]
        a1 = n[2 * tq:3 * tq] - lam_full * n[3 * tq:4 * tq]
        o = jnp.where(left, a0, a1)
        sq = o * o
        ms0 = jnp.sum(jnp.where(left, sq, 0.0), axis=-1, keepdims=True) / DIFF_DV
        ms1 = jnp.sum(jnp.where(left, 0.0, sq), axis=-1, keepdims=True) / DIFF_DV
        ms = jnp.where(left, ms0, ms1)
        y = o * lax.rsqrt(ms + EPS) * g_ref[...] * (1.0 - linit)
        o_ref[pl.ds(pl.multiple_of(qi * tq, tq), tq), :] = y.astype(o_ref.dtype)

    def query_block(qi, carry):
        nfull = lax.div(qi * tq, tk)
        npair = lax.div(nfull, 2)

        def step(t, cur, nxt):
            qk(2 * t + 2, nxt[0])
            qk(2 * t + 3, nxt[1])
            softmax_pv(qi, [(cur[0], 2 * t, False), (cur[1], 2 * t + 1, False)])

        def pair(t, c):
            lax.cond(lax.rem(t, 2) == 0, lambda: step(t, bufs[0], bufs[1]), lambda: step(t, bufs[1], bufs[0]))
            return c

        lax.fori_loop(0, npair, pair, 0)

        def tail(cur):
            has_full = lax.rem(nfull, 2) == 1

            @pl.when(has_full)
            def _():
                softmax_pv(qi, [(cur[0], nfull - 1, False), (cur[1], nfull, True)])

            @pl.when(jnp.logical_not(has_full))
            def _():
                softmax_pv(qi, [(cur[0], nfull, True)])

        lax.cond(lax.rem(npair, 2) == 0, lambda: tail(bufs[0]), lambda: tail(bufs[1]))
        finish_block(qi)
        start_block(jnp.minimum(qi + 1, nq - 1))
        return carry

    start_block(0)
    lax.fori_loop(0, nq, query_block, 0)


def diff_attention(proj, lam, linit, subln_g, *, batch, seq, tq=256, tk=256):
    t = batch * seq
    g2 = jnp.concatenate([subln_g, subln_g]).reshape(1, LANES).astype(F32)
    kern = functools.partial(_diff_kernel, tq=tq, tk=tk, nk=seq // tk, nq=seq // tq)
    fixed = lambda b, h: (0, 0)
    sbuf = pltpu.VMEM((4 * tq, tk), F32)
    return pl.pallas_call(
        kern,
        out_shape=jax.ShapeDtypeStruct((t, DIFF_HEADS * DIFF_DV), BF16),
        grid=(batch, DIFF_HEADS // 2),
        in_specs=[
            pl.BlockSpec((4, DIFF_DQK), fixed),
            pl.BlockSpec((1, 1), fixed),
            pl.BlockSpec((1, LANES), fixed),
            pl.BlockSpec((seq, LANES), fixed),
            pl.BlockSpec((seq, LANES), lambda b, h: (b, C_DQ // LANES + h)),
            pl.BlockSpec((seq, LANES), lambda b, h: (b, C_DK // LANES + h)),
            pl.BlockSpec((seq, LANES), lambda b, h: (b, C_DV // LANES + h)),
        ],
        out_specs=pl.BlockSpec((seq, LANES), lambda b, h: (b, h)),
        scratch_shapes=[sbuf, sbuf, sbuf, sbuf,
                        pltpu.VMEM((4 * tq, LANES), F32),
                        pltpu.VMEM((4 * tq, 2 * LANES), F32),
                        pltpu.VMEM((4 * tq, 2 * LANES), BF16)],
        compiler_params=_cparams(("parallel", "parallel")),
        name="diff_attention",
    )(lam.astype(F32), linit, g2, _diff_key_features(seq), proj, proj, proj)


def _swa_bias_tables():
    w = SWA_WINDOW
    rper = SWA_HEADS // SWA_KV_HEADS
    i = np.arange(w)[:, None]
    j = np.arange(2 * w)[None, :]
    dist = (i + w - j).astype(np.float64)
    valid = (dist >= 0) & (dist < w)
    slopes = 2.0 ** (-8.0 * np.arange(1, SWA_HEADS + 1, dtype=np.float64) / SWA_HEADS)
    out = np.empty((2, SWA_KV_HEADS, rper * w, 2 * w), np.float32)
    for variant, ok in enumerate((valid & (j >= w), valid)):
        for g in range(SWA_KV_HEADS):
            for r in range(rper):
                out[variant, g, r * w:(r + 1) * w] = np.where(ok, LOG2E * slopes[g * rper + r] * dist, 1e30)
    return jnp.asarray(out)


def _swa_kernel(b0_ref, b1_ref, sk_ref, q_ref, kp_ref, kc_ref, vp_ref, vc_ref, o_ref):
    g = pl.program_id(1)
    w = SWA_WINDOW
    rper = SWA_HEADS // SWA_KV_HEADS
    lane = lax.broadcasted_iota(jnp.int32, (3 * w, LANES), 1)
    mine = (lane // SWA_DH) == g

    def dup(prev_ref, cur_ref):
        x = jnp.concatenate([prev_ref[...], cur_ref[...]], axis=0).astype(F32)
        return jnp.where(mine, x, pltpu.roll(x, SWA_DH, axis=1)).astype(BF16)

    kk = dup(kp_ref, kc_ref)
    vv = jnp.concatenate([dup(vp_ref, vc_ref), jnp.ones((3 * w, LANES), BF16)], axis=1)
    qlane = lax.broadcasted_iota(jnp.int32, (w, LANES), 1)
    left = qlane < SWA_DH
    sink = sk_ref[0]

    for blk, b_ref in enumerate((b0_ref, b1_ref)):
        r0 = blk * w
        parts = []
        for p in range(rper // 2):
            q2 = q_ref[r0:r0 + w, p * LANES:(p + 1) * LANES]
            zero = jnp.zeros_like(q2)
            parts += [jnp.where(left, q2, zero), jnp.where(left, zero, q2)]
        qm = jnp.concatenate(parts, axis=0)
        s = lax.dot_general(qm, kk[r0:r0 + 2 * w], (((1,), (1,)), ((), ())), preferred_element_type=F32)
        u = s - b_ref[0, 0]
        m = jnp.maximum(jnp.max(u, axis=-1, keepdims=True), sink)
        pr = jnp.exp2(u - jnp.concatenate([m, m], axis=1))
        acc = jnp.dot(pr.astype(BF16), vv[r0:r0 + 2 * w], preferred_element_type=F32)
        l = acc[:, LANES:2 * LANES] + jnp.exp2(sink - m)
        o = acc[:, 0:LANES] / l
        for p in range(rper // 2):
            pair = jnp.where(left, o[(2 * p) * w:(2 * p + 1) * w], o[(2 * p + 1) * w:(2 * p + 2) * w])
            o_ref[r0:r0 + w, p * LANES:(p + 1) * LANES] = pair.astype(o_ref.dtype)


def swa_attention(proj, sinks, *, batch, seq):
    t = batch * seq
    w = SWA_WINDOW
    nb2 = seq // (2 * w)
    rper = SWA_HEADS // SWA_KV_HEADS
    bias = _swa_bias_tables()
    sink_rows = jnp.broadcast_to((sinks.astype(F32) * LOG2E).reshape(SWA_KV_HEADS, rper, 1, 1),
                                 (SWA_KV_HEADS, rper, w, LANES)).reshape(SWA_KV_HEADS, rper * w, LANES)
    qw = rper * SWA_DH
    kcol = C_SK // LANES
    vcol = C_SV // LANES

    def prev(b, n):
        return 2 * (b * nb2 + n) - jnp.where(n > 0, 1, 0)

    return pl.pallas_call(
        _swa_kernel,
        out_shape=jax.ShapeDtypeStruct((t, SWA_HEADS * SWA_DH), BF16),
        grid=(batch, SWA_KV_HEADS, nb2),
        in_specs=[
            pl.BlockSpec((1, 1, rper * w, 2 * w), lambda b, g, n: (jnp.minimum(n, 1), g, 0, 0)),
            pl.BlockSpec((1, 1, rper * w, 2 * w), lambda b, g, n: (1, g, 0, 0)),
            pl.BlockSpec((1, rper * w, LANES), lambda b, g, n: (g, 0, 0)),
            pl.BlockSpec((2 * w, qw), lambda b, g, n: (b * nb2 + n, C_SQ // qw + g)),
            pl.BlockSpec((w, LANES), lambda b, g, n: (prev(b, n), kcol)),
            pl.BlockSpec((2 * w, LANES), lambda b, g, n: (b * nb2 + n, kcol)),
            pl.BlockSpec((w, LANES), lambda b, g, n: (prev(b, n), vcol)),
            pl.BlockSpec((2 * w, LANES), lambda b, g, n: (b * nb2 + n, vcol)),
        ],
        out_specs=pl.BlockSpec((2 * w, qw), lambda b, g, n: (b * nb2 + n, g)),
        compiler_params=_cparams(("parallel", "parallel", "arbitrary")),
        name="swa_attention",
    )(bias, bias, sink_rows, proj, proj, proj, proj, proj)


def _gla_masks(tb):
    c = GLA_CHUNK
    t = np.arange(tb)[:, None]
    s = np.arange(tb)[None, :]
    same = (t // c) == (s // c)
    tri = same & (s <= t)
    return jnp.asarray(np.concatenate([tri, same], axis=0).astype(np.float32)).astype(BF16)


def _gla_kernel(mask_ref, w2_ref, gb_ref, ng_ref, q_ref, k_ref, v_ref, og_ref, lr_ref, o_ref, st_ref, *, tb):
    c = GLA_CHUNK
    nchunk = tb // c

    @pl.when(pl.program_id(2) == 0)
    def _():
        st_ref[...] = jnp.zeros_like(st_ref)

    z = jnp.dot(lr_ref[...], w2_ref[...], preferred_element_type=F32) + gb_ref[...]
    log_a = (jnp.minimum(z, 0.0) - jnp.log1p(jnp.exp(-jnp.abs(z)))) / GLA_TAU

    hi = log_a.astype(BF16)
    lo = (log_a - hi.astype(F32)).astype(BF16)
    hl = jnp.concatenate([hi, lo], axis=1)
    cs = jnp.dot(mask_ref[...], hl, preferred_element_type=F32)
    b = cs[0:tb, 0:LANES] + cs[0:tb, LANES:2 * LANES]
    b_last = cs[tb:2 * tb, 0:LANES] + cs[tb:2 * tb, LANES:2 * LANES]

    qf = q_ref[...].astype(F32)
    kf = k_ref[...].astype(F32)
    q_dec = (qf * jnp.exp(b)).astype(BF16)
    k_inv = (kf * jnp.exp(-b)).astype(BF16)
    k_end = (kf * jnp.exp(b_last - b)).astype(BF16)
    decay = jnp.exp(b_last)

    lane = lax.broadcasted_iota(jnp.int32, (tb, LANES), 1)
    left = lane < GLA_DK
    zero = jnp.zeros_like(q_dec)
    qd = [jnp.where(left, q_dec, zero), jnp.where(left, zero, q_dec)]
    tri = mask_ref[0:tb, :] > 0
    v = v_ref[...]

    intra = []
    for h in range(2):
        a = lax.dot_general(qd[h], k_inv, (((1,), (1,)), ((), ())), preferred_element_type=F32)
        a = jnp.where(tri, a, 0.0).astype(BF16)
        intra.append(jnp.dot(a, v[:, h * GLA_DV:(h + 1) * GLA_DV], preferred_element_type=F32))

    srow = lax.broadcasted_iota(jnp.int32, (2 * GLA_DV, LANES), 0) // GLA_DV
    scol = lax.broadcasted_iota(jnp.int32, (2 * GLA_DV, LANES), 1) // GLA_DK
    own = srow == scol
    state = st_ref[...]
    inter = []
    for n in range(nchunk):
        r0, r1 = n * c, (n + 1) * c
        inter.append(lax.dot_general(q_dec[r0:r1], state.astype(BF16), (((1,), (1,)), ((), ())),
                                     preferred_element_type=F32))
        kv_t = lax.dot_general(v[r0:r1], k_end[r0:r1], (((0,), (0,)), ((), ())),
                               preferred_element_type=F32)
        state = state * decay[r0:r0 + 1] + jnp.where(own, kv_t, 0.0)
    st_ref[...] = state
    o_inter = jnp.concatenate(inter, axis=0)

    og = og_ref[...].astype(F32)
    gate = og / (1.0 + jnp.exp(-og))
    for h in range(2):
        o = intra[h] + o_inter[:, h * GLA_DV:(h + 1) * GLA_DV]
        ms = jnp.mean(o * o, axis=-1, keepdims=True)
        y = o * lax.rsqrt(ms + EPS) * ng_ref[...] * gate[:, h * GLA_DV:(h + 1) * GLA_DV]
        o_ref[:, h * GLA_DV:(h + 1) * GLA_DV] = y.astype(o_ref.dtype)


def gla_attention(proj, w2p, gate_b, norm_g, *, batch, seq, tb=512):
    tb = min(tb, seq)
    t = batch * seq
    nblk = seq // tb
    masks = _gla_masks(tb)
    kern = functools.partial(_gla_kernel, tb=tb)
    w256 = 2 * GLA_DV
    return pl.pallas_call(
        kern,
        out_shape=jax.ShapeDtypeStruct((t, GLA_HEADS * GLA_DV), BF16),
        grid=(batch, GLA_HEADS // 2, nblk),
        in_specs=[
            pl.BlockSpec((2 * tb, tb), lambda b, h, n: (0, 0)),
            pl.BlockSpec((LANES, LANES), lambda b, h, n: (0, h)),
            pl.BlockSpec((1, LANES), lambda b, h, n: (0, h)),
            pl.BlockSpec((1, GLA_DV), lambda b, h, n: (0, 0)),
            pl.BlockSpec((tb, LANES), lambda b, h, n: (b * nblk + n, C_GQ // LANES + h)),
            pl.BlockSpec((tb, LANES), lambda b, h, n: (b * nblk + n, C_GK // LANES + h)),
            pl.BlockSpec((tb, w256), lambda b, h, n: (b * nblk + n, C_GV // w256 + h)),
            pl.BlockSpec((tb, w256), lambda b, h, n: (b * nblk + n, C_OG // w256 + h)),
            pl.BlockSpec((tb, LANES), lambda b, h, n: (b * nblk + n, C_LR // LANES)),
        ],
        out_specs=pl.BlockSpec((tb, w256), lambda b, h, n: (b * nblk + n, h)),
        scratch_shapes=[pltpu.VMEM((2 * GLA_DV, LANES), F32)],
        compiler_params=_cparams(("parallel", "parallel", "arbitrary")),
        name="gla_attention",
    )(masks, w2p, gate_b.reshape(1, -1).astype(F32), norm_g.reshape(1, -1).astype(F32),
      proj, proj, proj, proj, proj)


def _mix_out_kernel(yd_ref, ys_ref, yg_ref, w_ref, h_ref, o_ref):
    kd, ks = yd_ref.shape[1], ys_ref.shape[1]
    acc = jnp.dot(yd_ref[...], w_ref[0:kd, :], preferred_element_type=F32)
    acc += jnp.dot(ys_ref[...], w_ref[kd:kd + ks, :], preferred_element_type=F32)
    acc += jnp.dot(yg_ref[...], w_ref[kd + ks:, :], preferred_element_type=F32)
    o_ref[...] = h_ref[...] + acc


def mix_out(yd, ys, yg, w, layer, h, *, tm):
    m, d = h.shape
    row = lambda i: (i, 0)
    return pl.pallas_call(
        _mix_out_kernel,
        out_shape=jax.ShapeDtypeStruct((m, d), F32),
        grid=(m // tm,),
        in_specs=[
            pl.BlockSpec((tm, yd.shape[1]), row),
            pl.BlockSpec((tm, ys.shape[1]), row),
            pl.BlockSpec((tm, yg.shape[1]), row),
            pl.BlockSpec((None,) + w.shape[1:], lambda i: (layer, 0, 0)),
            pl.BlockSpec((tm, d), row),
        ],
        out_specs=pl.BlockSpec((tm, d), row),
        compiler_params=_cparams(("parallel",)),
        name="mix_out",
    )(yd, ys, yg, w, h)


def _xattn_kernel(h_ref, g_ref, wq_ref, kv_ref, wo_ref, gf_ref, o_ref, xn_ref):
    x = h_ref[...]
    ms = jnp.mean(x * x, axis=-1, keepdims=True)
    xn = (x * lax.rsqrt(ms + EPS) * g_ref[...]).astype(BF16)
    q = jnp.dot(xn, wq_ref[...], preferred_element_type=F32).astype(BF16)
    d_xa = XA_HEADS * XA_DH
    outs = []
    for hd in range(XA_HEADS):
        kh = kv_ref[:, hd * XA_DH:(hd + 1) * XA_DH]
        vh = kv_ref[:, d_xa + hd * XA_DH:d_xa + (hd + 1) * XA_DH]
        s = lax.dot_general(q[:, hd * XA_DH:(hd + 1) * XA_DH], kh, (((1,), (1,)), ((), ())),
                            preferred_element_type=F32)
        m = jnp.max(s, axis=-1, keepdims=True)
        p = jnp.exp2(s - m)
        l = jnp.sum(p, axis=-1, keepdims=True)
        outs.append((jnp.dot(p.astype(BF16), vh, preferred_element_type=F32) / l).astype(BF16))
    o = jnp.concatenate(outs, axis=1)
    y = x + jnp.dot(o, wo_ref[...], preferred_element_type=F32)
    o_ref[...] = y
    ms2 = jnp.mean(y * y, axis=-1, keepdims=True)
    xn_ref[...] = (y * lax.rsqrt(ms2 + EPS) * gf_ref[...]).astype(BF16)


def cross_attention(h, g, wq, kv, wo, layer, g_ffn, *, batch, seq, n_mem, tm):
    m, d = h.shape
    nt = seq // tm
    fixed = lambda b, i: (0, 0)
    row = lambda b, i: (b * nt + i, 0)
    return pl.pallas_call(
        _xattn_kernel,
        out_shape=(jax.ShapeDtypeStruct((m, d), F32), jax.ShapeDtypeStruct((m, d), BF16)),
        grid=(batch, nt),
        in_specs=[
            pl.BlockSpec((tm, d), row),
            pl.BlockSpec((1, d), fixed),
            pl.BlockSpec((None,) + wq.shape[1:], lambda b, i: (layer, 0, 0)),
            pl.BlockSpec((n_mem, kv.shape[1]), lambda b, i: (b, 0)),
            pl.BlockSpec((None,) + wo.shape[1:], lambda b, i: (layer, 0, 0)),
            pl.BlockSpec((1, d), fixed),
        ],
        out_specs=(pl.BlockSpec((tm, d), row), pl.BlockSpec((tm, d), row)),
        compiler_params=_cparams(("parallel", "arbitrary")),
        name="cross_attention",
    )(h, g.reshape(1, d), wq, kv, wo, g_ffn.reshape(1, d))


def _ffn_up_kernel(x_ref, xp_ref, wg_ref, wu_ref, cw_ref, cb_ref, o_ref,
                   xe_ref, ga_ref, ua_ref, gb_ref, ub_ref, xs_ref, ys_ref, *, tm, tiles_per_seq, nf, d_ff):
    i = pl.program_id(0)
    j = pl.program_id(1)
    gp = FFN_GROUP
    nv = gp // 8
    ck = FFN_CHUNK
    ts = FFN_STRIP
    ng = ts // gp
    nstrip = tm // ts
    d = x_ref.shape[1]

    def permuted(x, n):
        for c in range(d // LANES):
            xs_ref[c, 0:n, :] = x[:, c * LANES:(c + 1) * LANES]
        rows = []
        for g in range(n // gp):
            for v in range(nv):
                rows.append(jnp.concatenate(
                    [xs_ref[c, pl.ds(gp * g + v, 8, stride=nv), :] for c in range(d // LANES)], axis=1))
        return jnp.concatenate(rows, axis=0)

    @pl.when(j == 0)
    def _():
        first = (i % tiles_per_seq) == 0
        prev = permuted(xp_ref[...].astype(F32), gp)
        xe_ref[0:gp, :] = jnp.where(first, jnp.zeros_like(prev), prev).astype(BF16)
        for k in range(nstrip):
            xe_ref[gp + k * ts:gp + (k + 1) * ts, :] = permuted(
                x_ref[k * ts:(k + 1) * ts, :].astype(F32), ts).astype(BF16)
        gb_ref[...] = jnp.zeros_like(gb_ref)
        ub_ref[...] = jnp.zeros_like(ub_ref)

    def matmuls(c0, g_out, u_out, k):
        r0 = 0 if k == 0 else gp + k * ts
        r1 = gp + (k + 1) * ts
        xs = xe_ref[r0:r1, :]
        n = (r1 - r0) // gp
        g_out[r0 // gp:r1 // gp] = jnp.dot(xs, wg_ref[:, c0:c0 + ck], preferred_element_type=F32).reshape(n, gp, ck)
        u_out[r0 // gp:r1 // gp] = jnp.dot(xs, wu_ref[:, c0:c0 + ck], preferred_element_type=F32).reshape(n, gp, ck)

    def conv(h_ref, col, k):
        g0 = 1 + k * ng
        cw = cw_ref[:, pl.ds(col, ck)]
        taps = [cw[t:t + 1].reshape(1, 1, ck) for t in range(CONV_W)]
        bias = cb_ref[:, pl.ds(col, ck)].reshape(1, 1, ck)
        cur = [h_ref[g0:g0 + ng, 8 * v:8 * v + 8, :] for v in range(nv)]

        def carried(v):
            x = jnp.concatenate([h_ref[g0 - 1:g0, 8 * v:8 * v + 8, :], cur[v]], axis=0).reshape((ng + 1) * 8, ck)
            return x[7:7 + ng * 8].reshape(ng, 8, ck)

        a1 = carried(nv - 1)
        a2 = carried(nv - 2)
        prev1 = [a1] + cur[:nv - 1]
        prev2 = [a2, a1] + cur[:nv - 2]
        return [prev2[v] * taps[0] + prev1[v] * taps[1] + cur[v] * taps[2] + bias for v in range(nv)]

    def epilogue(g_in, u_in, col, k):
        col = pl.multiple_of(col, ck)
        gate = conv(g_in, col, k)
        up = conv(u_in, pl.multiple_of(d_ff + col, ck), k)
        for v in range(nv):
            act = gate[v] / (1.0 + jnp.exp(-gate[v])) * up[v]
            for g in range(ng):
                for c in range(ck // LANES):
                    ys_ref[c, pl.ds(gp * g + v, 8, stride=nv), :] = act[g, :, c * LANES:(c + 1) * LANES]
        nat = jnp.concatenate([ys_ref[c] for c in range(ck // LANES)], axis=1)
        o_ref[k * ts:(k + 1) * ts, pl.ds(col, ck)] = nat.astype(o_ref.dtype)

    col_a = j * (2 * ck)
    col_prev = jnp.maximum(col_a - ck, 0)
    for k in range(nstrip):
        matmuls(0, ga_ref, ua_ref, k)
        epilogue(gb_ref, ub_ref, col_prev, k)
    for k in range(nstrip):
        matmuls(ck, gb_ref, ub_ref, k)
        epilogue(ga_ref, ua_ref, col_a, k)

    @pl.when(j == nf - 1)
    def _():
        for k in range(nstrip):
            epilogue(gb_ref, ub_ref, col_a + ck, k)


def ffn_up(xn, w_up, conv_w, conv_b, layer, *, seq, tm):
    m, d = xn.shape
    d_ff = w_up.shape[2] // 2
    tf = 2 * FFN_CHUNK
    nf = d_ff // tf
    gp = FFN_GROUP
    kern = functools.partial(_ffn_up_kernel, tm=tm, tiles_per_seq=seq // tm, nf=nf, d_ff=d_ff)
    raw = pltpu.VMEM(((tm + gp) // gp, gp, FFN_CHUNK), F32)
    return pl.pallas_call(
        kern,
        out_shape=jax.ShapeDtypeStruct((m, d_ff), BF16),
        grid=(m // tm, nf),
        in_specs=[
            pl.BlockSpec((tm, d), lambda i, j: (i, 0)),
            pl.BlockSpec((gp, d), lambda i, j: (jnp.maximum(i * (tm // gp) - 1, 0), 0)),
            pl.BlockSpec((None, d, tf), lambda i, j: (layer, 0, j)),
            pl.BlockSpec((None, d, tf), lambda i, j: (layer, 0, nf + j)),
            pl.BlockSpec((None, CONV_W, 2 * d_ff), lambda i, j: (layer, 0, 0)),
            pl.BlockSpec((None, 1, 2 * d_ff), lambda i, j: (layer, 0, 0)),
        ],
        out_specs=pl.BlockSpec((tm, d_ff), lambda i, j: (i, 0)),
        scratch_shapes=[pltpu.VMEM((tm + gp, d), BF16), raw, raw, raw, raw,
                        pltpu.VMEM((d // LANES, FFN_STRIP, LANES), F32),
                        pltpu.VMEM((FFN_CHUNK // LANES, FFN_STRIP, LANES), F32)],
        compiler_params=_cparams(("parallel", "arbitrary")),
        name="ffn_up",
    )(xn, xn, w_up, w_up, conv_w, conv_b.reshape(conv_b.shape[0], 1, -1))


def _matmul_res_kernel(a_ref, w_ref, r_ref, o_ref):
    o_ref[...] = r_ref[...] + jnp.dot(a_ref[...], w_ref[...], preferred_element_type=F32)


def matmul_residual(a, w, layer, res, *, tm, tn):
    m, k = a.shape
    n = w.shape[2]
    return pl.pallas_call(
        _matmul_res_kernel,
        out_shape=jax.ShapeDtypeStruct((m, n), F32),
        grid=(m // tm, n // tn),
        in_specs=[
            pl.BlockSpec((tm, k), lambda i, j: (i, 0)),
            pl.BlockSpec((None, k, tn), lambda i, j: (layer, 0, j)),
            pl.BlockSpec((tm, tn), lambda i, j: (i, j)),
        ],
        out_specs=pl.BlockSpec((tm, tn), lambda i, j: (i, j)),
        compiler_params=_cparams(("parallel", "arbitrary")),
        name="matmul_residual",
    )(a, w, res)


def _rmsnorm_kernel(x_ref, g_ref, o_ref):
    x = x_ref[...]
    ms = jnp.mean(x * x, axis=-1, keepdims=True)
    o_ref[...] = x * lax.rsqrt(ms + EPS) * g_ref[...]


def rmsnorm(x, g, *, tm):
    m, d = x.shape
    return pl.pallas_call(
        _rmsnorm_kernel,
        out_shape=jax.ShapeDtypeStruct((m, d), F32),
        grid=(m // tm,),
        in_specs=[pl.BlockSpec((tm, d), lambda i: (i, 0)), pl.BlockSpec((1, d), lambda i: (0, 0))],
        out_specs=pl.BlockSpec((tm, d), lambda i: (i, 0)),
        compiler_params=_cparams(("parallel",)),
        name="final_rmsnorm",
    )(x, g.reshape(1, d))


def _prep_w_in(w):
    depth, d = w.shape[0], w.shape[1]
    scale = jnp.ones((C_OG,), F32)
    scale = scale.at[C_DQ:C_DK].set(DIFF_DQK ** -0.5 * LOG2E)
    scale = scale.at[C_SQ:C_SK].set(SWA_DH ** -0.5 * LOG2E)
    scale = scale.at[C_GQ:C_GK].set(GLA_DK ** -0.5)
    lr0 = C_OG
    og0 = lr0 + GLA_RANK
    main = w[:, :, :C_OG] * scale
    og = w[:, :, og0:og0 + GLA_HEADS * GLA_DV]
    lr = w[:, :, lr0:lr0 + GLA_RANK]
    pad = jnp.zeros((depth, d, N_PROJ - C_LR - GLA_RANK), w.dtype)
    return jnp.concatenate([main, og, lr, pad], axis=2).astype(BF16)


def kernel(x, mem, norm_mix_g, w_in, diff_lambda, diff_subln_g, swa_sinks, gla_gate_w2, gla_gate_b, gla_norm_g, w_out, norm_xa_g, norm_mem_g, xa_wq, xa_wkv, xa_wo, norm_ffn_g, ffn_w_up, ffn_conv_w, ffn_conv_b, ffn_w_down, final_norm_g):
    batch, seq, d = x.shape
    n_mem = mem.shape[1]
    depth = w_in.shape[0]
    t = batch * seq
    tm = min(512, seq)
    tm_big = min(1024, seq)

    h = x.reshape(t, d)
    memf = mem.reshape(batch * n_mem, d)

    w_in_b = _prep_w_in(w_in)
    w_out_b = w_out.astype(BF16)
    wq_b = (xa_wq * (XA_DH ** -0.5 * LOG2E)).astype(BF16)
    wkv_b = xa_wkv.astype(BF16)
    wo_b = xa_wo.astype(BF16)
    w_up_b = ffn_w_up.astype(BF16)
    w_down_b = ffn_w_down.astype(BF16)

    for l in range(depth):
        lambda_init = 0.8 - 0.6 * math.exp(-0.3 * l)
        linit = jnp.full((1, 1), lambda_init, F32)

        proj = norm_matmul(h, norm_mix_g[l], w_in_b, l, tm=tm_big, tn=1536)
        y_diff = diff_attention(proj, diff_lambda[l], linit, diff_subln_g[l], batch=batch, seq=seq)
        y_swa = swa_attention(proj, swa_sinks[l], batch=batch, seq=seq)
        w2p = jnp.zeros((LANES, GLA_HEADS * GLA_DK), F32).at[:GLA_RANK].set(gla_gate_w2[l]).astype(BF16)
        y_gla = gla_attention(proj, w2p, gla_gate_b[l], gla_norm_g[l], batch=batch, seq=seq)
        h = mix_out(y_diff, y_swa, y_gla, w_out_b, l, h, tm=tm)

        kv = norm_matmul(memf, norm_mem_g[l], wkv_b, l, tm=min(512, batch * n_mem), tn=512)
        h, xn = cross_attention(h, norm_xa_g[l], wq_b, kv, wo_b, l, norm_ffn_g[l],
                                batch=batch, seq=seq, n_mem=n_mem, tm=tm)

        act = ffn_up(xn, w_up_b, ffn_conv_w, ffn_conv_b, l, seq=seq, tm=tm_big)
        h = matmul_residual(act, w_down_b, l, h, tm=tm_big, tn=512)

    out = rmsnorm(h, final_norm_g, tm=tm)
    return out.reshape(batch, seq, d)
```

```python
import functools
import math

import jax
import jax.numpy as jnp
import numpy as np
from jax import lax
from jax.experimental import pallas as pl
from jax.experimental.pallas import tpu as pltpu

F32 = jnp.float32
BF16 = jnp.bfloat16
EPS = 1e-6
NEG = -1e30
LOG2E = math.log2(math.e)

LANES = 128
VMEM_LIMIT = 56 * 1024 * 1024

DIFF_HEADS = 8
DIFF_DQK = 32
DIFF_DV = 64
SWA_HEADS = 16
SWA_KV_HEADS = 2
SWA_DH = 64
SWA_WINDOW = 128
GLA_HEADS = 4
GLA_DK = 64
GLA_DV = 128
GLA_RANK = 16
GLA_TAU = 16.0
GLA_CHUNK = 64
XA_HEADS = 4
XA_DH = 128
CONV_W = 3
CONV_HALO = 16
FFN_CHUNK = 256

C_DQ, C_DK, C_DV = 0, 512, 1024
C_SQ, C_SK, C_SV = 1536, 2560, 2688
C_GQ, C_GK, C_GV = 2816, 3072, 3328
C_OG, C_LR = 3840, 4352
N_PROJ = 4608


def _cparams(sem):
    return pltpu.CompilerParams(dimension_semantics=sem, vmem_limit_bytes=VMEM_LIMIT)


def _norm_matmul_kernel(x_ref, g_ref, w_ref, o_ref, xn_ref):
    @pl.when(pl.program_id(1) == 0)
    def _():
        x = x_ref[...]
        ms = jnp.mean(x * x, axis=-1, keepdims=True)
        xn_ref[...] = (x * lax.rsqrt(ms + EPS) * g_ref[...]).astype(BF16)

    o_ref[...] = jnp.dot(xn_ref[...], w_ref[...], preferred_element_type=F32).astype(o_ref.dtype)


def norm_matmul(x, g, w, layer, *, tm, tn, out_dtype=BF16):
    m, k = x.shape
    n = w.shape[2]
    return pl.pallas_call(
        _norm_matmul_kernel,
        out_shape=jax.ShapeDtypeStruct((m, n), out_dtype),
        grid=(m // tm, n // tn),
        in_specs=[
            pl.BlockSpec((tm, k), lambda i, j: (i, 0)),
            pl.BlockSpec((1, k), lambda i, j: (0, 0)),
            pl.BlockSpec((None, k, tn), lambda i, j: (layer, 0, j)),
        ],
        out_specs=pl.BlockSpec((tm, tn), lambda i, j: (i, j)),
        scratch_shapes=[pltpu.VMEM((tm, k), BF16)],
        compiler_params=_cparams(("parallel", "arbitrary")),
        name="norm_matmul",
    )(x, g.reshape(1, k), w)


def _bf16_split3(x):
    parts = []
    r = np.float64(x)
    for _ in range(3):
        bits = np.array([r], np.float32).view(np.uint32)
        bits = (bits + np.uint32(0x7FFF) + ((bits >> np.uint32(16)) & np.uint32(1))) & np.uint32(0xFFFF0000)
        a = float(bits.view(np.float32)[0])
        parts.append(a)
        r = r - a
    return parts


LOG2E_PARTS = _bf16_split3(LOG2E)


def _diff_key_features(seq):
    j = np.arange(seq)
    f = np.zeros((seq, LANES), np.float32)
    for t in range(3):
        f[:, 2 * t] = j // LANES
        f[:, 2 * t + 1] = j % LANES
    f[:, 6] = 1.0
    f[:, 7] = 1.0
    return jnp.asarray(f).astype(BF16)


def _diff_kernel(lam_ref, linit_ref, g_ref, kf_ref, q_ref, k_ref, v_ref, o_ref,
                 s0_ref, s1_ref, s2_ref, s3_ref, m_ref, acc_ref, q4_ref, *, tq, tk, nk, nq):
    hp = pl.program_id(1)
    rows = 4 * tq
    lane = lax.broadcasted_iota(jnp.int32, (tq, LANES), 1)
    left = lane < DIFF_DV
    ones = jnp.ones((tk, LANES), BF16)
    bufs = ((s0_ref, s1_ref), (s2_ref, s3_ref))

    lamv = lam_ref[...]
    lam1 = jnp.exp(jnp.sum(lamv[0:1] * lamv[1:2], axis=-1, keepdims=True))
    lam2 = jnp.exp(jnp.sum(lamv[2:3] * lamv[3:4], axis=-1, keepdims=True))
    linit = linit_ref[...]
    lam_full = lam1 - lam2 + linit

    def qk(kj, s_out):
        start = pl.multiple_of(jnp.minimum(kj, nk - 1) * tk, tk)
        ka = jnp.concatenate([k_ref[pl.ds(start, tk), :], kf_ref[pl.ds(start, tk), :]], axis=1)
        s_out[...] = lax.dot_general(q4_ref[...], ka, (((1,), (1,)), ((), ())), preferred_element_type=F32)

    def start_block(qi):
        pos = qi * tq + lax.broadcasted_iota(jnp.int32, (tq, LANES), 0)
        i_hi = (pos // LANES).astype(F32)
        i_lo = (pos % LANES).astype(F32)

        def features(hl):
            head = (2 * hp + hl + 1).astype(F32)
            slope = jnp.exp2(jnp.zeros((tq, LANES), F32) - head * (8.0 / DIFF_HEADS))
            f = jnp.zeros((tq, LANES), F32)
            for t, part in enumerate(LOG2E_PARTS):
                f = jnp.where(lane == 2 * t, slope * (part * LANES), f)
                f = jnp.where(lane == 2 * t + 1, slope * part, f)
            f = jnp.where(lane == 6, -slope * (LOG2E * LANES) * i_hi, f)
            f = jnp.where(lane == 7, -slope * LOG2E * i_lo, f)
            return f.astype(BF16)

        q = q_ref[pl.ds(pl.multiple_of(qi * tq, tq), tq), :]
        zero = jnp.zeros_like(q)
        feats = [features(0), features(1)]
        for c in range(4):
            q4_ref[c * tq:(c + 1) * tq, :] = jnp.concatenate(
                [jnp.where((lane // DIFF_DQK) == c, q, zero), feats[c // 2]], axis=1)
        m_ref[...] = jnp.full_like(m_ref, NEG)
        acc_ref[...] = jnp.zeros_like(acc_ref)
        qk(0, s0_ref)
        qk(1, s1_ref)

    def values(kj):
        start = pl.multiple_of(kj * tk, tk)
        return jnp.concatenate([v_ref[pl.ds(start, tk), :], ones], axis=1)

    def softmax_pv(qi, blocks):
        def causal(s, kj):
            ii = qi * tq + lax.broadcasted_iota(jnp.int32, (rows, tk), 0) % tq
            jj = kj * tk + lax.broadcasted_iota(jnp.int32, (rows, tk), 1)
            return jnp.where(ii >= jj, s, NEG)

        ss = [causal(s_in[...], kj) if masked else s_in[...] for s_in, kj, masked in blocks]
        smax = ss[0] if len(ss) == 1 else jnp.maximum(ss[0], ss[1])
        m_old = m_ref[...]
        m_new = jnp.maximum(m_old, jnp.max(smax, axis=-1, keepdims=True))
        m_rep = jnp.concatenate([m_new] * (tk // LANES), axis=1)
        p = jnp.concatenate([jnp.exp2(s - m_rep).astype(BF16) for s in ss], axis=1)
        va = jnp.concatenate([values(kj) for _, kj, _ in blocks], axis=0)
        alpha = jnp.exp2(m_old - m_new)
        acc_ref[...] = (jnp.concatenate([alpha, alpha], axis=1) * acc_ref[...]
                        + jnp.dot(p, va, preferred_element_type=F32))
        m_ref[...] = m_new

    def finish_block(qi):
        acc = acc_ref[...]
        n = acc[:, 0:LANES] / acc[:, LANES:2 * LANES]
        a0 = n[0:tq] - lam_full * n[tq:2 * tq]
        a1 = n[2 * tq:3 * tq] - lam_full * n[3 * tq:4 * tq]
        o = jnp.where(left, a0, a1)
        sq = o * o
        ms0 = jnp.sum(jnp.where(left, sq, 0.0), axis=-1, keepdims=True) / DIFF_DV
        ms1 = jnp.sum(jnp.where(left, 0.0, sq), axis=-1, keepdims=True) / DIFF_DV
        ms = jnp.where(left, ms0, ms1)
        y = o * lax.rsqrt(ms + EPS) * g_ref[...] * (1.0 - linit)
        o_ref[pl.ds(pl.multiple_of(qi * tq, tq), tq), :] = y.astype(o_ref.dtype)

    def query_block(qi, carry):
        nfull = lax.div(qi * tq, tk)
        npair = lax.div(nfull, 2)

        def step(t, cur, nxt):
            qk(2 * t + 2, nxt[0])
            qk(2 * t + 3, nxt[1])
            softmax_pv(qi, [(cur[0], 2 * t, False), (cur[1], 2 * t + 1, False)])

        def pair(t, c):
            lax.cond(lax.rem(t, 2) == 0, lambda: step(t, bufs[0], bufs[1]), lambda: step(t, bufs[1], bufs[0]))
            return c

        lax.fori_loop(0, npair, pair, 0)

        def tail(cur):
            has_full = lax.rem(nfull, 2) == 1

            @pl.when(has_full)
            def _():
                softmax_pv(qi, [(cur[0], nfull - 1, False), (cur[1], nfull, True)])

            @pl.when(jnp.logical_not(has_full))
            def _():
                softmax_pv(qi, [(cur[0], nfull, True)])

        lax.cond(lax.rem(npair, 2) == 0, lambda: tail(bufs[0]), lambda: tail(bufs[1]))
        finish_block(qi)
        start_block(jnp.minimum(qi + 1, nq - 1))
        return carry

    start_block(0)
    lax.fori_loop(0, nq, query_block, 0)


def diff_attention(proj, lam, linit, subln_g, *, batch, seq, tq=256, tk=256):
    t = batch * seq
    g2 = jnp.concatenate([subln_g, subln_g]).reshape(1, LANES).astype(F32)
    kern = functools.partial(_diff_kernel, tq=tq, tk=tk, nk=seq // tk, nq=seq // tq)
    fixed = lambda b, h: (0, 0)
    sbuf = pltpu.VMEM((4 * tq, tk), F32)
    return pl.pallas_call(
        kern,
        out_shape=jax.ShapeDtypeStruct((t, DIFF_HEADS * DIFF_DV), BF16),
        grid=(batch, DIFF_HEADS // 2),
        in_specs=[
            pl.BlockSpec((4, DIFF_DQK), fixed),
            pl.BlockSpec((1, 1), fixed),
            pl.BlockSpec((1, LANES), fixed),
            pl.BlockSpec((seq, LANES), fixed),
            pl.BlockSpec((seq, LANES), lambda b, h: (b, C_DQ // LANES + h)),
            pl.BlockSpec((seq, LANES), lambda b, h: (b, C_DK // LANES + h)),
            pl.BlockSpec((seq, LANES), lambda b, h: (b, C_DV // LANES + h)),
        ],
        out_specs=pl.BlockSpec((seq, LANES), lambda b, h: (b, h)),
        scratch_shapes=[sbuf, sbuf, sbuf, sbuf,
                        pltpu.VMEM((4 * tq, LANES), F32),
                        pltpu.VMEM((4 * tq, 2 * LANES), F32),
                        pltpu.VMEM((4 * tq, 2 * LANES), BF16)],
        compiler_params=_cparams(("parallel", "parallel")),
        name="diff_attention",
    )(lam.astype(F32), linit, g2, _diff_key_features(seq), proj, proj, proj)


def _swa_bias_tables():
    w = SWA_WINDOW
    rper = SWA_HEADS // SWA_KV_HEADS
    i = np.arange(w)[:, None]
    j = np.arange(2 * w)[None, :]
    dist = (i + w - j).astype(np.float64)
    valid = (dist >= 0) & (dist < w)
    slopes = 2.0 ** (-8.0 * np.arange(1, SWA_HEADS + 1, dtype=np.float64) / SWA_HEADS)
    out = np.empty((2, SWA_KV_HEADS, rper * w, 2 * w), np.float32)
    for variant, ok in enumerate((valid & (j >= w), valid)):
        for g in range(SWA_KV_HEADS):
            for r in range(rper):
                out[variant, g, r * w:(r + 1) * w] = np.where(ok, LOG2E * slopes[g * rper + r] * dist, 1e30)
    return jnp.asarray(out)


def _swa_kernel(b0_ref, b1_ref, sk_ref, q_ref, kp_ref, kc_ref, vp_ref, vc_ref, o_ref):
    g = pl.program_id(1)
    w = SWA_WINDOW
    rper = SWA_HEADS // SWA_KV_HEADS
    lane = lax.broadcasted_iota(jnp.int32, (3 * w, LANES), 1)
    mine = (lane // SWA_DH) == g

    def dup(prev_ref, cur_ref):
        x = jnp.concatenate([prev_ref[...], cur_ref[...]], axis=0).astype(F32)
        return jnp.where(mine, x, pltpu.roll(x, SWA_DH, axis=1)).astype(BF16)

    kk = dup(kp_ref, kc_ref)
    vv = jnp.concatenate([dup(vp_ref, vc_ref), jnp.ones((3 * w, LANES), BF16)], axis=1)
    qlane = lax.broadcasted_iota(jnp.int32, (w, LANES), 1)
    left = qlane < SWA_DH
    sink = sk_ref[0]

    for blk, b_ref in enumerate((b0_ref, b1_ref)):
        r0 = blk * w
        parts = []
        for p in range(rper // 2):
            q2 = q_ref[r0:r0 + w, p * LANES:(p + 1) * LANES]
            zero = jnp.zeros_like(q2)
            parts += [jnp.where(left, q2, zero), jnp.where(left, zero, q2)]
        qm = jnp.concatenate(parts, axis=0)
        s = lax.dot_general(qm, kk[r0:r0 + 2 * w], (((1,), (1,)), ((), ())), preferred_element_type=F32)
        u = s - b_ref[0, 0]
        m = jnp.maximum(jnp.max(u, axis=-1, keepdims=True), sink)
        pr = jnp.exp2(u - jnp.concatenate([m, m], axis=1))
        acc = jnp.dot(pr.astype(BF16), vv[r0:r0 + 2 * w], preferred_element_type=F32)
        l = acc[:, LANES:2 * LANES] + jnp.exp2(sink - m)
        o = acc[:, 0:LANES] / l
        for p in range(rper // 2):
            pair = jnp.where(left, o[(2 * p) * w:(2 * p + 1) * w], o[(2 * p + 1) * w:(2 * p + 2) * w])
            o_ref[r0:r0 + w, p * LANES:(p + 1) * LANES] = pair.astype(o_ref.dtype)


def swa_attention(proj, sinks, *, batch, seq):
    t = batch * seq
    w = SWA_WINDOW
    nb2 = seq // (2 * w)
    rper = SWA_HEADS // SWA_KV_HEADS
    bias = _swa_bias_tables()
    sink_rows = jnp.broadcast_to((sinks.astype(F32) * LOG2E).reshape(SWA_KV_HEADS, rper, 1, 1),
                                 (SWA_KV_HEADS, rper, w, LANES)).reshape(SWA_KV_HEADS, rper * w, LANES)
    qw = rper * SWA_DH
    kcol = C_SK // LANES
    vcol = C_SV // LANES

    def prev(b, n):
        return 2 * (b * nb2 + n) - jnp.where(n > 0, 1, 0)

    return pl.pallas_call(
        _swa_kernel,
        out_shape=jax.ShapeDtypeStruct((t, SWA_HEADS * SWA_DH), BF16),
        grid=(batch, SWA_KV_HEADS, nb2),
        in_specs=[
            pl.BlockSpec((1, 1, rper * w, 2 * w), lambda b, g, n: (jnp.minimum(n, 1), g, 0, 0)),
            pl.BlockSpec((1, 1, rper * w, 2 * w), lambda b, g, n: (1, g, 0, 0)),
            pl.BlockSpec((1, rper * w, LANES), lambda b, g, n: (g, 0, 0)),
            pl.BlockSpec((2 * w, qw), lambda b, g, n: (b * nb2 + n, C_SQ // qw + g)),
            pl.BlockSpec((w, LANES), lambda b, g, n: (prev(b, n), kcol)),
            pl.BlockSpec((2 * w, LANES), lambda b, g, n: (b * nb2 + n, kcol)),
            pl.BlockSpec((w, LANES), lambda b, g, n: (prev(b, n), vcol)),
            pl.BlockSpec((2 * w, LANES), lambda b, g, n: (b * nb2 + n, vcol)),
        ],
        out_specs=pl.BlockSpec((2 * w, qw), lambda b, g, n: (b * nb2 + n, g)),
        compiler_params=_cparams(("parallel", "parallel", "arbitrary")),
        name="swa_attention",
    )(bias, bias, sink_rows, proj, proj, proj, proj, proj)


def _gla_masks(tb):
    c = GLA_CHUNK
    t = np.arange(tb)[:, None]
    s = np.arange(tb)[None, :]
    same = (t // c) == (s // c)
    tri = same & (s <= t)
    return jnp.asarray(np.concatenate([tri, same], axis=0).astype(np.float32)).astype(BF16)


def _gla_kernel(mask_ref, w2_ref, gb_ref, ng_ref, q_ref, k_ref, v_ref, og_ref, lr_ref, o_ref, st_ref, *, tb):
    c = GLA_CHUNK
    nchunk = tb // c

    @pl.when(pl.program_id(2) == 0)
    def _():
        st_ref[...] = jnp.zeros_like(st_ref)

    z = jnp.dot(lr_ref[...], w2_ref[...], preferred_element_type=F32) + gb_ref[...]
    log_a = (jnp.minimum(z, 0.0) - jnp.log1p(jnp.exp(-jnp.abs(z)))) / GLA_TAU

    hi = log_a.astype(BF16)
    lo = (log_a - hi.astype(F32)).astype(BF16)
    hl = jnp.concatenate([hi, lo], axis=1)
    cs = jnp.dot(mask_ref[...], hl, preferred_element_type=F32)
    b = cs[0:tb, 0:LANES] + cs[0:tb, LANES:2 * LANES]
    b_last = cs[tb:2 * tb, 0:LANES] + cs[tb:2 * tb, LANES:2 * LANES]

    qf = q_ref[...].astype(F32)
    kf = k_ref[...].astype(F32)
    q_dec = (qf * jnp.exp(b)).astype(BF16)
    k_inv = (kf * jnp.exp(-b)).astype(BF16)
    k_end = (kf * jnp.exp(b_last - b)).astype(BF16)
    decay = jnp.exp(b_last)

    lane = lax.broadcasted_iota(jnp.int32, (tb, LANES), 1)
    left = lane < GLA_DK
    zero = jnp.zeros_like(q_dec)
    qd = [jnp.where(left, q_dec, zero), jnp.where(left, zero, q_dec)]
    tri = mask_ref[0:tb, :] > 0
    v = v_ref[...]

    intra = []
    for h in range(2):
        a = lax.dot_general(qd[h], k_inv, (((1,), (1,)), ((), ())), preferred_element_type=F32)
        a = jnp.where(tri, a, 0.0).astype(BF16)
        intra.append(jnp.dot(a, v[:, h * GLA_DV:(h + 1) * GLA_DV], preferred_element_type=F32))

    srow = lax.broadcasted_iota(jnp.int32, (2 * GLA_DV, LANES), 0) // GLA_DV
    scol = lax.broadcasted_iota(jnp.int32, (2 * GLA_DV, LANES), 1) // GLA_DK
    own = srow == scol
    state = st_ref[...]
    inter = []
    for n in range(nchunk):
        r0, r1 = n * c, (n + 1) * c
        inter.append(lax.dot_general(q_dec[r0:r1], state.astype(BF16), (((1,), (1,)), ((), ())),
                                     preferred_element_type=F32))
        kv_t = lax.dot_general(v[r0:r1], k_end[r0:r1], (((0,), (0,)), ((), ())),
                               preferred_element_type=F32)
        state = state * decay[r0:r0 + 1] + jnp.where(own, kv_t, 0.0)
    st_ref[...] = state
    o_inter = jnp.concatenate(inter, axis=0)

    og = og_ref[...].astype(F32)
    gate = og / (1.0 + jnp.exp(-og))
    for h in range(2):
        o = intra[h] + o_inter[:, h * GLA_DV:(h + 1) * GLA_DV]
        ms = jnp.mean(o * o, axis=-1, keepdims=True)
        y = o * lax.rsqrt(ms + EPS) * ng_ref[...] * gate[:, h * GLA_DV:(h + 1) * GLA_DV]
        o_ref[:, h * GLA_DV:(h + 1) * GLA_DV] = y.astype(o_ref.dtype)


def gla_attention(proj, w2p, gate_b, norm_g, *, batch, seq, tb=512):
    tb = min(tb, seq)
    t = batch * seq
    nblk = seq // tb
    masks = _gla_masks(tb)
    kern = functools.partial(_gla_kernel, tb=tb)
    w256 = 2 * GLA_DV
    return pl.pallas_call(
        kern,
        out_shape=jax.ShapeDtypeStruct((t, GLA_HEADS * GLA_DV), BF16),
        grid=(batch, GLA_HEADS // 2, nblk),
        in_specs=[
            pl.BlockSpec((2 * tb, tb), lambda b, h, n: (0, 0)),
            pl.BlockSpec((LANES, LANES), lambda b, h, n: (0, h)),
            pl.BlockSpec((1, LANES), lambda b, h, n: (0, h)),
            pl.BlockSpec((1, GLA_DV), lambda b, h, n: (0, 0)),
            pl.BlockSpec((tb, LANES), lambda b, h, n: (b * nblk + n, C_GQ // LANES + h)),
            pl.BlockSpec((tb, LANES), lambda b, h, n: (b * nblk + n, C_GK // LANES + h)),
            pl.BlockSpec((tb, w256), lambda b, h, n: (b * nblk + n, C_GV // w256 + h)),
            pl.BlockSpec((tb, w256), lambda b, h, n: (b * nblk + n, C_OG // w256 + h)),
            pl.BlockSpec((tb, LANES), lambda b, h, n: (b * nblk + n, C_LR // LANES)),
        ],
        out_specs=pl.BlockSpec((tb, w256), lambda b, h, n: (b * nblk + n, h)),
        scratch_shapes=[pltpu.VMEM((2 * GLA_DV, LANES), F32)],
        compiler_params=_cparams(("parallel", "parallel", "arbitrary")),
        name="gla_attention",
    )(masks, w2p, gate_b.reshape(1, -1).astype(F32), norm_g.reshape(1, -1).astype(F32),
      proj, proj, proj, proj, proj)


def _mix_out_kernel(yd_ref, ys_ref, yg_ref, w_ref, h_ref, o_ref):
    kd, ks = yd_ref.shape[1], ys_ref.shape[1]
    acc = jnp.dot(yd_ref[...], w_ref[0:kd, :], preferred_element_type=F32)
    acc += jnp.dot(ys_ref[...], w_ref[kd:kd + ks, :], preferred_element_type=F32)
    acc += jnp.dot(yg_ref[...], w_ref[kd + ks:, :], preferred_element_type=F32)
    o_ref[...] = h_ref[...] + acc


def mix_out(yd, ys, yg, w, layer, h, *, tm):
    m, d = h.shape
    row = lambda i: (i, 0)
    return pl.pallas_call(
        _mix_out_kernel,
        out_shape=jax.ShapeDtypeStruct((m, d), F32),
        grid=(m // tm,),
        in_specs=[
            pl.BlockSpec((tm, yd.shape[1]), row),
            pl.BlockSpec((tm, ys.shape[1]), row),
            pl.BlockSpec((tm, yg.shape[1]), row),
            pl.BlockSpec((None,) + w.shape[1:], lambda i: (layer, 0, 0)),
            pl.BlockSpec((tm, d), row),
        ],
        out_specs=pl.BlockSpec((tm, d), row),
        compiler_params=_cparams(("parallel",)),
        name="mix_out",
    )(yd, ys, yg, w, h)


def _xattn_kernel(h_ref, g_ref, wq_ref, kv_ref, wo_ref, gf_ref, o_ref, xn_ref):
    x = h_ref[...]
    ms = jnp.mean(x * x, axis=-1, keepdims=True)
    xn = (x * lax.rsqrt(ms + EPS) * g_ref[...]).astype(BF16)
    q = jnp.dot(xn, wq_ref[...], preferred_element_type=F32).astype(BF16)
    d_xa = XA_HEADS * XA_DH
    outs = []
    for hd in range(XA_HEADS):
        kh = kv_ref[:, hd * XA_DH:(hd + 1) * XA_DH]
        vh = kv_ref[:, d_xa + hd * XA_DH:d_xa + (hd + 1) * XA_DH]
        s = lax.dot_general(q[:, hd * XA_DH:(hd + 1) * XA_DH], kh, (((1,), (1,)), ((), ())),
                            preferred_element_type=F32)
        m = jnp.max(s, axis=-1, keepdims=True)
        p = jnp.exp2(s - m)
        l = jnp.sum(p, axis=-1, keepdims=True)
        outs.append((jnp.dot(p.astype(BF16), vh, preferred_element_type=F32) / l).astype(BF16))
    o = jnp.concatenate(outs, axis=1)
    y = x + jnp.dot(o, wo_ref[...], preferred_element_type=F32)
    o_ref[...] = y
    ms2 = jnp.mean(y * y, axis=-1, keepdims=True)
    xn_ref[...] = (y * lax.rsqrt(ms2 + EPS) * gf_ref[...]).astype(BF16)


def cross_attention(h, g, wq, kv, wo, layer, g_ffn, *, batch, seq, n_mem, tm):
    m, d = h.shape
    nt = seq // tm
    fixed = lambda b, i: (0, 0)
    row = lambda b, i: (b * nt + i, 0)
    return pl.pallas_call(
        _xattn_kernel,
        out_shape=(jax.ShapeDtypeStruct((m, d), F32), jax.ShapeDtypeStruct((m, d), BF16)),
        grid=(batch, nt),
        in_specs=[
            pl.BlockSpec((tm, d), row),
            pl.BlockSpec((1, d), fixed),
            pl.BlockSpec((None,) + wq.shape[1:], lambda b, i: (layer, 0, 0)),
            pl.BlockSpec((n_mem, kv.shape[1]), lambda b, i: (b, 0)),
            pl.BlockSpec((None,) + wo.shape[1:], lambda b, i: (layer, 0, 0)),
            pl.BlockSpec((1, d), fixed),
        ],
        out_specs=(pl.BlockSpec((tm, d), row), pl.BlockSpec((tm, d), row)),
        compiler_params=_cparams(("parallel", "arbitrary")),
        name="cross_attention",
    )(h, g.reshape(1, d), wq, kv, wo, g_ffn.reshape(1, d))


def _ffn_up_kernel(x_ref, xp_ref, wg_ref, wu_ref, cw_ref, cb_ref, o_ref, xe_ref, ga_ref, ua_ref, gb_ref, ub_ref,
                   *, tm, tiles_per_seq, nf, d_ff):
    i = pl.program_id(0)
    j = pl.program_id(1)
    hl = CONV_HALO
    ck = FFN_CHUNK

    @pl.when(j == 0)
    def _():
        first = (i % tiles_per_seq) == 0
        prev = xp_ref[...]
        xe_ref[0:hl, :] = jnp.where(first, jnp.zeros_like(prev), prev)
        xe_ref[hl:hl + tm, :] = x_ref[...]
        gb_ref[...] = jnp.zeros_like(gb_ref)
        ub_ref[...] = jnp.zeros_like(ub_ref)

    xe = xe_ref[...]

    def matmuls(c0, g_out, u_out):
        g_out[...] = jnp.dot(xe, wg_ref[:, c0:c0 + ck], preferred_element_type=F32)
        u_out[...] = jnp.dot(xe, wu_ref[:, c0:c0 + ck], preferred_element_type=F32)

    def conv(h_ref, col):
        hh = h_ref[...]
        cw = cw_ref[:, pl.ds(col, ck)]
        return (hh[hl - 2:hl - 2 + tm] * cw[0:1] + hh[hl - 1:hl - 1 + tm] * cw[1:2]
                + hh[hl:hl + tm] * cw[2:3] + cb_ref[:, pl.ds(col, ck)])

    def epilogue(g_in, u_in, col):
        col = pl.multiple_of(col, ck)
        gate = conv(g_in, col)
        up = conv(u_in, pl.multiple_of(d_ff + col, ck))
        o_ref[:, pl.ds(col, ck)] = (gate / (1.0 + jnp.exp(-gate)) * up).astype(o_ref.dtype)

    col_a = j * (2 * ck)
    matmuls(0, ga_ref, ua_ref)
    epilogue(gb_ref, ub_ref, jnp.maximum(col_a - ck, 0))
    matmuls(ck, gb_ref, ub_ref)
    epilogue(ga_ref, ua_ref, col_a)

    @pl.when(j == nf - 1)
    def _():
        epilogue(gb_ref, ub_ref, col_a + ck)


def ffn_up(xn, w_up, conv_w, conv_b, layer, *, seq, tm):
    m, d = xn.shape
    d_ff = w_up.shape[2] // 2
    tf = 2 * FFN_CHUNK
    nf = d_ff // tf
    hl = CONV_HALO
    kern = functools.partial(_ffn_up_kernel, tm=tm, tiles_per_seq=seq // tm, nf=nf, d_ff=d_ff)
    raw = pltpu.VMEM((tm + hl, FFN_CHUNK), F32)
    return pl.pallas_call(
        kern,
        out_shape=jax.ShapeDtypeStruct((m, d_ff), BF16),
        grid=(m // tm, nf),
        in_specs=[
            pl.BlockSpec((tm, d), lambda i, j: (i, 0)),
            pl.BlockSpec((hl, d), lambda i, j: (jnp.maximum(i * (tm // hl) - 1, 0), 0)),
            pl.BlockSpec((None, d, tf), lambda i, j: (layer, 0, j)),
            pl.BlockSpec((None, d, tf), lambda i, j: (layer, 0, nf + j)),
            pl.BlockSpec((None, CONV_W, 2 * d_ff), lambda i, j: (layer, 0, 0)),
            pl.BlockSpec((None, 1, 2 * d_ff), lambda i, j: (layer, 0, 0)),
        ],
        out_specs=pl.BlockSpec((tm, d_ff), lambda i, j: (i, 0)),
        scratch_shapes=[pltpu.VMEM((tm + hl, d), BF16), raw, raw, raw, raw],
        compiler_params=_cparams(("parallel", "arbitrary")),
        name="ffn_up",
    )(xn, xn, w_up, w_up, conv_w, conv_b.reshape(conv_b.shape[0], 1, -1))


def _matmul_res_kernel(a_ref, w_ref, r_ref, o_ref):
    o_ref[...] = r_ref[...] + jnp.dot(a_ref[...], w_ref[...], preferred_element_type=F32)


def matmul_residual(a, w, layer, res, *, tm, tn):
    m, k = a.shape
    n = w.shape[2]
    return pl.pallas_call(
        _matmul_res_kernel,
        out_shape=jax.ShapeDtypeStruct((m, n), F32),
        grid=(m // tm, n // tn),
        in_specs=[
            pl.BlockSpec((tm, k), lambda i, j: (i, 0)),
            pl.BlockSpec((None, k, tn), lambda i, j: (layer, 0, j)),
            pl.BlockSpec((tm, tn), lambda i, j: (i, j)),
        ],
        out_specs=pl.BlockSpec((tm, tn), lambda i, j: (i, j)),
        compiler_params=_cparams(("parallel", "arbitrary")),
        name="matmul_residual",
    )(a, w, res)


def _rmsnorm_kernel(x_ref, g_ref, o_ref):
    x = x_ref[...]
    ms = jnp.mean(x * x, axis=-1, keepdims=True)
    o_ref[...] = x * lax.rsqrt(ms + EPS) * g_ref[...]


def rmsnorm(x, g, *, tm):
    m, d = x.shape
    return pl.pallas_call(
        _rmsnorm_kernel,
        out_shape=jax.ShapeDtypeStruct((m, d), F32),
        grid=(m // tm,),
        in_specs=[pl.BlockSpec((tm, d), lambda i: (i, 0)), pl.BlockSpec((1, d), lambda i: (0, 0))],
        out_specs=pl.BlockSpec((tm, d), lambda i: (i, 0)),
        compiler_params=_cparams(("parallel",)),
        name="final_rmsnorm",
    )(x, g.reshape(1, d))


def _prep_w_in(w):
    depth, d = w.shape[0], w.shape[1]
    scale = jnp.ones((C_OG,), F32)
    scale = scale.at[C_DQ:C_DK].set(DIFF_DQK ** -0.5 * LOG2E)
    scale = scale.at[C_SQ:C_SK].set(SWA_DH ** -0.5 * LOG2E)
    scale = scale.at[C_GQ:C_GK].set(GLA_DK ** -0.5)
    lr0 = C_OG
    og0 = lr0 + GLA_RANK
    main = w[:, :, :C_OG] * scale
    og = w[:, :, og0:og0 + GLA_HEADS * GLA_DV]
    lr = w[:, :, lr0:lr0 + GLA_RANK]
    pad = jnp.zeros((depth, d, N_PROJ - C_LR - GLA_RANK), w.dtype)
    return jnp.concatenate([main, og, lr, pad], axis=2).astype(BF16)


def kernel(x, mem, norm_mix_g, w_in, diff_lambda, diff_subln_g, swa_sinks, gla_gate_w2, gla_gate_b, gla_norm_g, w_out, norm_xa_g, norm_mem_g, xa_wq, xa_wkv, xa_wo, norm_ffn_g, ffn_w_up, ffn_conv_w, ffn_conv_b, ffn_w_down, final_norm_g):
    batch, seq, d = x.shape
    n_mem = mem.shape[1]
    depth = w_in.shape[0]
    t = batch * seq
    tm = min(512, seq)
    tm_big = min(1024, seq)

    h = x.reshape(t, d)
    memf = mem.reshape(batch * n_mem, d)

    w_in_b = _prep_w_in(w_in)
    w_out_b = w_out.astype(BF16)
    wq_b = (xa_wq * (XA_DH ** -0.5 * LOG2E)).astype(BF16)
    wkv_b = xa_wkv.astype(BF16)
    wo_b = xa_wo.astype(BF16)
    w_up_b = ffn_w_up.astype(BF16)
    w_down_b = ffn_w_down.astype(BF16)

    for l in range(depth):
        lambda_init = 0.8 - 0.6 * math.exp(-0.3 * l)
        linit = jnp.full((1, 1), lambda_init, F32)

        proj = norm_matmul(h, norm_mix_g[l], w_in_b, l, tm=tm_big, tn=1536)
        y_diff = diff_attention(proj, diff_lambda[l], linit, diff_subln_g[l], batch=batch, seq=seq)
        y_swa = swa_attention(proj, swa_sinks[l], batch=batch, seq=seq)
        w2p = jnp.zeros((LANES, GLA_HEADS * GLA_DK), F32).at[:GLA_RANK].set(gla_gate_w2[l]).astype(BF16)
        y_gla = gla_attention(proj, w2p, gla_gate_b[l], gla_norm_g[l], batch=batch, seq=seq)
        h = mix_out(y_diff, y_swa, y_gla, w_out_b, l, h, tm=tm)

        kv = norm_matmul(memf, norm_mem_g[l], wkv_b, l, tm=min(512, batch * n_mem), tn=512)
        h, xn = cross_attention(h, norm_xa_g[l], wq_b, kv, wo_b, l, norm_ffn_g[l],
                                batch=batch, seq=seq, n_mem=n_mem, tm=tm)

        act = ffn_up(xn, w_up_b, ffn_conv_w, ffn_conv_b, l, seq=seq, tm=tm_big)
        h = matmul_residual(act, w_down_b, l, h, tm=tm_big, tn=512)

    out = rmsnorm(h, final_norm_g, tm=tm)
    return out.reshape(batch, seq, d)
```

```python
import functools
import math

import jax
import jax.numpy as jnp
import numpy as np
from jax import lax
from jax.experimental import pallas as pl
from jax.experimental.pallas import tpu as pltpu

F32 = jnp.float32
BF16 = jnp.bfloat16
EPS = 1e-6
NEG = -1e30
LOG2E = math.log2(math.e)

LANES = 128
VMEM_LIMIT = 56 * 1024 * 1024

DIFF_HEADS = 8
DIFF_DQK = 32
DIFF_DV = 64
SWA_HEADS = 16
SWA_KV_HEADS = 2
SWA_DH = 64
SWA_WINDOW = 128
GLA_HEADS = 4
GLA_DK = 64
GLA_DV = 128
GLA_RANK = 16
GLA_TAU = 16.0
GLA_CHUNK = 64
XA_HEADS = 4
XA_DH = 128
CONV_W = 3
CONV_HALO = 16
FFN_CHUNK = 256

C_DQ, C_DK, C_DV = 0, 512, 1024
C_SQ, C_SK, C_SV = 1536, 2560, 2688
C_GQ, C_GK, C_GV = 2816, 3072, 3328
C_OG, C_LR = 3840, 4352
N_PROJ = 4608


def _cparams(sem):
    return pltpu.CompilerParams(dimension_semantics=sem, vmem_limit_bytes=VMEM_LIMIT)


def _norm_matmul_kernel(x_ref, g_ref, w_ref, o_ref, xn_ref):
    @pl.when(pl.program_id(1) == 0)
    def _():
        x = x_ref[...]
        ms = jnp.mean(x * x, axis=-1, keepdims=True)
        xn_ref[...] = (x * lax.rsqrt(ms + EPS) * g_ref[...]).astype(BF16)

    o_ref[...] = jnp.dot(xn_ref[...], w_ref[...], preferred_element_type=F32).astype(o_ref.dtype)


def norm_matmul(x, g, w, layer, *, tm, tn, out_dtype=BF16):
    m, k = x.shape
    n = w.shape[2]
    return pl.pallas_call(
        _norm_matmul_kernel,
        out_shape=jax.ShapeDtypeStruct((m, n), out_dtype),
        grid=(m // tm, n // tn),
        in_specs=[
            pl.BlockSpec((tm, k), lambda i, j: (i, 0)),
            pl.BlockSpec((1, k), lambda i, j: (0, 0)),
            pl.BlockSpec((None, k, tn), lambda i, j: (layer, 0, j)),
        ],
        out_specs=pl.BlockSpec((tm, tn), lambda i, j: (i, j)),
        scratch_shapes=[pltpu.VMEM((tm, k), BF16)],
        compiler_params=_cparams(("parallel", "arbitrary")),
        name="norm_matmul",
    )(x, g.reshape(1, k), w)


def _bf16_split3(x):
    parts = []
    r = np.float64(x)
    for _ in range(3):
        bits = np.array([r], np.float32).view(np.uint32)
        bits = (bits + np.uint32(0x7FFF) + ((bits >> np.uint32(16)) & np.uint32(1))) & np.uint32(0xFFFF0000)
        a = float(bits.view(np.float32)[0])
        parts.append(a)
        r = r - a
    return parts


LOG2E_PARTS = _bf16_split3(LOG2E)


def _diff_key_features(seq):
    j = np.arange(seq)
    f = np.zeros((seq, LANES), np.float32)
    for t in range(3):
        f[:, 2 * t] = j // LANES
        f[:, 2 * t + 1] = j % LANES
    f[:, 6] = 1.0
    f[:, 7] = 1.0
    return jnp.asarray(f).astype(BF16)


def _diff_kernel(lam_ref, linit_ref, g_ref, kf_ref, q_ref, k_ref, v_ref, o_ref,
                 s0_ref, s1_ref, s2_ref, s3_ref, m_ref, acc_ref, q4_ref, *, tq, tk, nk, nq):
    hp = pl.program_id(1)
    rows = 4 * tq
    lane = lax.broadcasted_iota(jnp.int32, (tq, LANES), 1)
    left = lane < DIFF_DV
    ones = jnp.ones((tk, LANES), BF16)
    bufs = ((s0_ref, s1_ref), (s2_ref, s3_ref))

    lamv = lam_ref[...]
    lam1 = jnp.exp(jnp.sum(lamv[0:1] * lamv[1:2], axis=-1, keepdims=True))
    lam2 = jnp.exp(jnp.sum(lamv[2:3] * lamv[3:4], axis=-1, keepdims=True))
    linit = linit_ref[...]
    lam_full = lam1 - lam2 + linit

    def qk(kj, s_out):
        start = pl.multiple_of(jnp.minimum(kj, nk - 1) * tk, tk)
        ka = jnp.concatenate([k_ref[pl.ds(start, tk), :], kf_ref[pl.ds(start, tk), :]], axis=1)
        s_out[...] = lax.dot_general(q4_ref[...], ka, (((1,), (1,)), ((), ())), preferred_element_type=F32)

    def start_block(qi):
        pos = qi * tq + lax.broadcasted_iota(jnp.int32, (tq, LANES), 0)
        i_hi = (pos // LANES).astype(F32)
        i_lo = (pos % LANES).astype(F32)

        def features(hl):
            head = (2 * hp + hl + 1).astype(F32)
            slope = jnp.exp2(jnp.zeros((tq, LANES), F32) - head * (8.0 / DIFF_HEADS))
            f = jnp.zeros((tq, LANES), F32)
            for t, part in enumerate(LOG2E_PARTS):
                f = jnp.where(lane == 2 * t, slope * (part * LANES), f)
                f = jnp.where(lane == 2 * t + 1, slope * part, f)
            f = jnp.where(lane == 6, -slope * (LOG2E * LANES) * i_hi, f)
            f = jnp.where(lane == 7, -slope * LOG2E * i_lo, f)
            return f.astype(BF16)

        q = q_ref[pl.ds(pl.multiple_of(qi * tq, tq), tq), :]
        zero = jnp.zeros_like(q)
        feats = [features(0), features(1)]
        for c in range(4):
            q4_ref[c * tq:(c + 1) * tq, :] = jnp.concatenate(
                [jnp.where((lane // DIFF_DQK) == c, q, zero), feats[c // 2]], axis=1)
        m_ref[...] = jnp.full_like(m_ref, NEG)
        acc_ref[...] = jnp.zeros_like(acc_ref)
        qk(0, s0_ref)
        qk(1, s1_ref)

    def values(kj):
        start = pl.multiple_of(kj * tk, tk)
        return jnp.concatenate([v_ref[pl.ds(start, tk), :], ones], axis=1)

    def softmax_pv(qi, blocks):
        def causal(s, kj):
            ii = qi * tq + lax.broadcasted_iota(jnp.int32, (rows, tk), 0) % tq
            jj = kj * tk + lax.broadcasted_iota(jnp.int32, (rows, tk), 1)
            return jnp.where(ii >= jj, s, NEG)

        ss = [causal(s_in[...], kj) if masked else s_in[...] for s_in, kj, masked in blocks]
        smax = ss[0] if len(ss) == 1 else jnp.maximum(ss[0], ss[1])
        m_old = m_ref[...]
        m_new = jnp.maximum(m_old, jnp.max(smax, axis=-1, keepdims=True))
        m_rep = jnp.concatenate([m_new] * (tk // LANES), axis=1)
        p = jnp.concatenate([jnp.exp2(s - m_rep).astype(BF16) for s in ss], axis=1)
        va = jnp.concatenate([values(kj) for _, kj, _ in blocks], axis=0)
        alpha = jnp.exp2(m_old - m_new)
        acc_ref[...] = (jnp.concatenate([alpha, alpha], axis=1) * acc_ref[...]
                        + jnp.dot(p, va, preferred_element_type=F32))
        m_ref[...] = m_new

    def finish_block(qi):
        acc = acc_ref[...]
        n = acc[:, 0:LANES] / acc[:, LANES:2 * LANES]
        a0 = n[0:tq] - lam_full * n[tq:2 * tq]
        a1 = n[2 * tq:3 * tq] - lam_full * n[3 * tq:4 * tq]
        o = jnp.where(left, a0, a1)
        sq = o * o
        ms0 = jnp.sum(jnp.where(left, sq, 0.0), axis=-1, keepdims=True) / DIFF_DV
        ms1 = jnp.sum(jnp.where(left, 0.0, sq), axis=-1, keepdims=True) / DIFF_DV
        ms = jnp.where(left, ms0, ms1)
        y = o * lax.rsqrt(ms + EPS) * g_ref[...] * (1.0 - linit)
        o_ref[pl.ds(pl.multiple_of(qi * tq, tq), tq), :] = y.astype(o_ref.dtype)

    def query_block(qi, carry):
        nfull = lax.div(qi * tq, tk)
        npair = lax.div(nfull, 2)

        def step(t, cur, nxt):
            qk(2 * t + 2, nxt[0])
            qk(2 * t + 3, nxt[1])
            softmax_pv(qi, [(cur[0], 2 * t, False), (cur[1], 2 * t + 1, False)])

        def pair(t, c):
            lax.cond(lax.rem(t, 2) == 0, lambda: step(t, bufs[0], bufs[1]), lambda: step(t, bufs[1], bufs[0]))
            return c

        lax.fori_loop(0, npair, pair, 0)

        def tail(cur):
            has_full = lax.rem(nfull, 2) == 1

            @pl.when(has_full)
            def _():
                softmax_pv(qi, [(cur[0], nfull - 1, False), (cur[1], nfull, True)])

            @pl.when(jnp.logical_not(has_full))
            def _():
                softmax_pv(qi, [(cur[0], nfull, True)])

        lax.cond(lax.rem(npair, 2) == 0, lambda: tail(bufs[0]), lambda: tail(bufs[1]))
        finish_block(qi)
        start_block(jnp.minimum(qi + 1, nq - 1))
        return carry

    start_block(0)
    lax.fori_loop(0, nq, query_block, 0)


def diff_attention(proj, lam, linit, subln_g, *, batch, seq, tq=256, tk=256):
    t = batch * seq
    g2 = jnp.concatenate([subln_g, subln_g]).reshape(1, LANES).astype(F32)
    kern = functools.partial(_diff_kernel, tq=tq, tk=tk, nk=seq // tk, nq=seq // tq)
    fixed = lambda b, h: (0, 0)
    sbuf = pltpu.VMEM((4 * tq, tk), F32)
    return pl.pallas_call(
        kern,
        out_shape=jax.ShapeDtypeStruct((t, DIFF_HEADS * DIFF_DV), BF16),
        grid=(batch, DIFF_HEADS // 2),
        in_specs=[
            pl.BlockSpec((4, DIFF_DQK), fixed),
            pl.BlockSpec((1, 1), fixed),
            pl.BlockSpec((1, LANES), fixed),
            pl.BlockSpec((seq, LANES), fixed),
            pl.BlockSpec((seq, LANES), lambda b, h: (b, C_DQ // LANES + h)),
            pl.BlockSpec((seq, LANES), lambda b, h: (b, C_DK // LANES + h)),
            pl.BlockSpec((seq, LANES), lambda b, h: (b, C_DV // LANES + h)),
        ],
        out_specs=pl.BlockSpec((seq, LANES), lambda b, h: (b, h)),
        scratch_shapes=[sbuf, sbuf, sbuf, sbuf,
                        pltpu.VMEM((4 * tq, LANES), F32),
                        pltpu.VMEM((4 * tq, 2 * LANES), F32),
                        pltpu.VMEM((4 * tq, 2 * LANES), BF16)],
        compiler_params=_cparams(("parallel", "parallel")),
        name="diff_attention",
    )(lam.astype(F32), linit, g2, _diff_key_features(seq), proj, proj, proj)


def _swa_bias_tables():
    w = SWA_WINDOW
    rper = SWA_HEADS // SWA_KV_HEADS
    i = np.arange(w)[:, None]
    j = np.arange(2 * w)[None, :]
    dist = (i + w - j).astype(np.float64)
    valid = (dist >= 0) & (dist < w)
    slopes = 2.0 ** (-8.0 * np.arange(1, SWA_HEADS + 1, dtype=np.float64) / SWA_HEADS)
    out = np.empty((2, SWA_KV_HEADS, rper * w, 2 * w), np.float32)
    for variant, ok in enumerate((valid & (j >= w), valid)):
        for g in range(SWA_KV_HEADS):
            for r in range(rper):
                out[variant, g, r * w:(r + 1) * w] = np.where(ok, LOG2E * slopes[g * rper + r] * dist, 1e30)
    return jnp.asarray(out)


def _swa_kernel(b0_ref, b1_ref, sk_ref, q_ref, kp_ref, kc_ref, vp_ref, vc_ref, o_ref):
    g = pl.program_id(1)
    w = SWA_WINDOW
    rper = SWA_HEADS // SWA_KV_HEADS
    lane = lax.broadcasted_iota(jnp.int32, (3 * w, LANES), 1)
    mine = (lane // SWA_DH) == g

    def dup(prev_ref, cur_ref):
        x = jnp.concatenate([prev_ref[...], cur_ref[...]], axis=0).astype(F32)
        return jnp.where(mine, x, pltpu.roll(x, SWA_DH, axis=1)).astype(BF16)

    kk = dup(kp_ref, kc_ref)
    vv = jnp.concatenate([dup(vp_ref, vc_ref), jnp.ones((3 * w, LANES), BF16)], axis=1)
    qlane = lax.broadcasted_iota(jnp.int32, (w, LANES), 1)
    left = qlane < SWA_DH
    sink = sk_ref[0]

    for blk, b_ref in enumerate((b0_ref, b1_ref)):
        r0 = blk * w
        parts = []
        for p in range(rper // 2):
            q2 = q_ref[r0:r0 + w, p * LANES:(p + 1) * LANES]
            zero = jnp.zeros_like(q2)
            parts += [jnp.where(left, q2, zero), jnp.where(left, zero, q2)]
        qm = jnp.concatenate(parts, axis=0)
        s = lax.dot_general(qm, kk[r0:r0 + 2 * w], (((1,), (1,)), ((), ())), preferred_element_type=F32)
        u = s - b_ref[0, 0]
        m = jnp.maximum(jnp.max(u, axis=-1, keepdims=True), sink)
        pr = jnp.exp2(u - jnp.concatenate([m, m], axis=1))
        acc = jnp.dot(pr.astype(BF16), vv[r0:r0 + 2 * w], preferred_element_type=F32)
        l = acc[:, LANES:2 * LANES] + jnp.exp2(sink - m)
        o = acc[:, 0:LANES] / l
        for p in range(rper // 2):
            pair = jnp.where(left, o[(2 * p) * w:(2 * p + 1) * w], o[(2 * p + 1) * w:(2 * p + 2) * w])
            o_ref[r0:r0 + w, p * LANES:(p + 1) * LANES] = pair.astype(o_ref.dtype)


def swa_attention(proj, sinks, *, batch, seq):
    t = batch * seq
    w = SWA_WINDOW
    nb2 = seq // (2 * w)
    rper = SWA_HEADS // SWA_KV_HEADS
    bias = _swa_bias_tables()
    sink_rows = jnp.broadcast_to((sinks.astype(F32) * LOG2E).reshape(SWA_KV_HEADS, rper, 1, 1),
                                 (SWA_KV_HEADS, rper, w, LANES)).reshape(SWA_KV_HEADS, rper * w, LANES)
    qw = rper * SWA_DH
    kcol = C_SK // LANES
    vcol = C_SV // LANES

    def prev(b, n):
        return 2 * (b * nb2 + n) - jnp.where(n > 0, 1, 0)

    return pl.pallas_call(
        _swa_kernel,
        out_shape=jax.ShapeDtypeStruct((t, SWA_HEADS * SWA_DH), BF16),
        grid=(batch, SWA_KV_HEADS, nb2),
        in_specs=[
            pl.BlockSpec((1, 1, rper * w, 2 * w), lambda b, g, n: (jnp.minimum(n, 1), g, 0, 0)),
            pl.BlockSpec((1, 1, rper * w, 2 * w), lambda b, g, n: (1, g, 0, 0)),
            pl.BlockSpec((1, rper * w, LANES), lambda b, g, n: (g, 0, 0)),
            pl.BlockSpec((2 * w, qw), lambda b, g, n: (b * nb2 + n, C_SQ // qw + g)),
            pl.BlockSpec((w, LANES), lambda b, g, n: (prev(b, n), kcol)),
            pl.BlockSpec((2 * w, LANES), lambda b, g, n: (b * nb2 + n, kcol)),
            pl.BlockSpec((w, LANES), lambda b, g, n: (prev(b, n), vcol)),
            pl.BlockSpec((2 * w, LANES), lambda b, g, n: (b * nb2 + n, vcol)),
        ],
        out_specs=pl.BlockSpec((2 * w, qw), lambda b, g, n: (b * nb2 + n, g)),
        compiler_params=_cparams(("parallel", "parallel", "arbitrary")),
        name="swa_attention",
    )(bias, bias, sink_rows, proj, proj, proj, proj, proj)


def _gla_masks(tb):
    c = GLA_CHUNK
    t = np.arange(tb)[:, None]
    s = np.arange(tb)[None, :]
    same = (t // c) == (s // c)
    tri = same & (s <= t)
    return jnp.asarray(np.concatenate([tri, same], axis=0).astype(np.float32)).astype(BF16)


def _gla_kernel(mask_ref, w2_ref, gb_ref, ng_ref, q_ref, k_ref, v_ref, og_ref, lr_ref, o_ref, st_ref, *, tb):
    c = GLA_CHUNK
    nchunk = tb // c

    @pl.when(pl.program_id(2) == 0)
    def _():
        st_ref[...] = jnp.zeros_like(st_ref)

    z = jnp.dot(lr_ref[...], w2_ref[...], preferred_element_type=F32) + gb_ref[...]
    log_a = (jnp.minimum(z, 0.0) - jnp.log1p(jnp.exp(-jnp.abs(z)))) / GLA_TAU

    hi = log_a.astype(BF16)
    lo = (log_a - hi.astype(F32)).astype(BF16)
    hl = jnp.concatenate([hi, lo], axis=1)
    cs = jnp.dot(mask_ref[...], hl, preferred_element_type=F32)
    b = cs[0:tb, 0:LANES] + cs[0:tb, LANES:2 * LANES]
    b_last = cs[tb:2 * tb, 0:LANES] + cs[tb:2 * tb, LANES:2 * LANES]

    qf = q_ref[...].astype(F32)
    kf = k_ref[...].astype(F32)
    q_dec = (qf * jnp.exp(b)).astype(BF16)
    k_inv = (kf * jnp.exp(-b)).astype(BF16)
    k_end = (kf * jnp.exp(b_last - b)).astype(BF16)
    decay = jnp.exp(b_last)

    lane = lax.broadcasted_iota(jnp.int32, (tb, LANES), 1)
    left = lane < GLA_DK
    zero = jnp.zeros_like(q_dec)
    qd = [jnp.where(left, q_dec, zero), jnp.where(left, zero, q_dec)]
    tri = mask_ref[0:tb, :] > 0
    v = v_ref[...]

    intra = []
    for h in range(2):
        a = lax.dot_general(qd[h], k_inv, (((1,), (1,)), ((), ())), preferred_element_type=F32)
        a = jnp.where(tri, a, 0.0).astype(BF16)
        intra.append(jnp.dot(a, v[:, h * GLA_DV:(h + 1) * GLA_DV], preferred_element_type=F32))

    srow = lax.broadcasted_iota(jnp.int32, (2 * GLA_DV, LANES), 0) // GLA_DV
    scol = lax.broadcasted_iota(jnp.int32, (2 * GLA_DV, LANES), 1) // GLA_DK
    own = srow == scol
    state = st_ref[...]
    inter = []
    for n in range(nchunk):
        r0, r1 = n * c, (n + 1) * c
        inter.append(lax.dot_general(q_dec[r0:r1], state.astype(BF16), (((1,), (1,)), ((), ())),
                                     preferred_element_type=F32))
        kv_t = lax.dot_general(v[r0:r1], k_end[r0:r1], (((0,), (0,)), ((), ())),
                               preferred_element_type=F32)
        state = state * decay[r0:r0 + 1] + jnp.where(own, kv_t, 0.0)
    st_ref[...] = state
    o_inter = jnp.concatenate(inter, axis=0)

    og = og_ref[...].astype(F32)
    gate = og / (1.0 + jnp.exp(-og))
    for h in range(2):
        o = intra[h] + o_inter[:, h * GLA_DV:(h + 1) * GLA_DV]
        ms = jnp.mean(o * o, axis=-1, keepdims=True)
        y = o * lax.rsqrt(ms + EPS) * ng_ref[...] * gate[:, h * GLA_DV:(h + 1) * GLA_DV]
        o_ref[:, h * GLA_DV:(h + 1) * GLA_DV] = y.astype(o_ref.dtype)


def gla_attention(proj, w2p, gate_b, norm_g, *, batch, seq, tb=512):
    tb = min(tb, seq)
    t = batch * seq
    nblk = seq // tb
    masks = _gla_masks(tb)
    kern = functools.partial(_gla_kernel, tb=tb)
    w256 = 2 * GLA_DV
    return pl.pallas_call(
        kern,
        out_shape=jax.ShapeDtypeStruct((t, GLA_HEADS * GLA_DV), BF16),
        grid=(batch, GLA_HEADS // 2, nblk),
        in_specs=[
            pl.BlockSpec((2 * tb, tb), lambda b, h, n: (0, 0)),
            pl.BlockSpec((LANES, LANES), lambda b, h, n: (0, h)),
            pl.BlockSpec((1, LANES), lambda b, h, n: (0, h)),
            pl.BlockSpec((1, GLA_DV), lambda b, h, n: (0, 0)),
            pl.BlockSpec((tb, LANES), lambda b, h, n: (b * nblk + n, C_GQ // LANES + h)),
            pl.BlockSpec((tb, LANES), lambda b, h, n: (b * nblk + n, C_GK // LANES + h)),
            pl.BlockSpec((tb, w256), lambda b, h, n: (b * nblk + n, C_GV // w256 + h)),
            pl.BlockSpec((tb, w256), lambda b, h, n: (b * nblk + n, C_OG // w256 + h)),
            pl.BlockSpec((tb, LANES), lambda b, h, n: (b * nblk + n, C_LR // LANES)),
        ],
        out_specs=pl.BlockSpec((tb, w256), lambda b, h, n: (b * nblk + n, h)),
        scratch_shapes=[pltpu.VMEM((2 * GLA_DV, LANES), F32)],
        compiler_params=_cparams(("parallel", "parallel", "arbitrary")),
        name="gla_attention",
    )(masks, w2p, gate_b.reshape(1, -1).astype(F32), norm_g.reshape(1, -1).astype(F32),
      proj, proj, proj, proj, proj)


def _mix_xattn_kernel(yd_ref, ys_ref, yg_ref, w_ref, h_ref, g_ref, wq_ref, kv_ref, wo_ref, gf_ref, o_ref, xn_ref):
    kd, ks = yd_ref.shape[1], ys_ref.shape[1]
    acc = jnp.dot(yd_ref[...], w_ref[0:kd, :], preferred_element_type=F32)
    acc += jnp.dot(ys_ref[...], w_ref[kd:kd + ks, :], preferred_element_type=F32)
    acc += jnp.dot(yg_ref[...], w_ref[kd + ks:, :], preferred_element_type=F32)
    x = h_ref[...] + acc
    ms = jnp.mean(x * x, axis=-1, keepdims=True)
    xn = (x * lax.rsqrt(ms + EPS) * g_ref[...]).astype(BF16)
    q = jnp.dot(xn, wq_ref[...], preferred_element_type=F32).astype(BF16)
    d_xa = XA_HEADS * XA_DH
    outs = []
    for hd in range(XA_HEADS):
        kh = kv_ref[:, hd * XA_DH:(hd + 1) * XA_DH]
        vh = kv_ref[:, d_xa + hd * XA_DH:d_xa + (hd + 1) * XA_DH]
        s = lax.dot_general(q[:, hd * XA_DH:(hd + 1) * XA_DH], kh, (((1,), (1,)), ((), ())),
                            preferred_element_type=F32)
        m = jnp.max(s, axis=-1, keepdims=True)
        p = jnp.exp2(s - m)
        l = jnp.sum(p, axis=-1, keepdims=True)
        outs.append((jnp.dot(p.astype(BF16), vh, preferred_element_type=F32) / l).astype(BF16))
    o = jnp.concatenate(outs, axis=1)
    y = x + jnp.dot(o, wo_ref[...], preferred_element_type=F32)
    o_ref[...] = y
    ms2 = jnp.mean(y * y, axis=-1, keepdims=True)
    xn_ref[...] = (y * lax.rsqrt(ms2 + EPS) * gf_ref[...]).astype(BF16)


def mix_xattn(yd, ys, yg, w_out, h, g, wq, kv, wo, layer, g_ffn, *, batch, seq, n_mem, tm):
    m, d = h.shape
    nt = seq // tm
    fixed = lambda b, i: (0, 0)
    row = lambda b, i: (b * nt + i, 0)
    once = pl.Buffered(1)
    wspec = lambda w: pl.BlockSpec((None,) + w.shape[1:], lambda b, i: (layer, 0, 0), pipeline_mode=once)
    return pl.pallas_call(
        _mix_xattn_kernel,
        out_shape=(jax.ShapeDtypeStruct((m, d), F32), jax.ShapeDtypeStruct((m, d), BF16)),
        grid=(batch, nt),
        in_specs=[
            pl.BlockSpec((tm, yd.shape[1]), row),
            pl.BlockSpec((tm, ys.shape[1]), row),
            pl.BlockSpec((tm, yg.shape[1]), row),
            wspec(w_out),
            pl.BlockSpec((tm, d), row),
            pl.BlockSpec((1, d), fixed),
            wspec(wq),
            pl.BlockSpec((n_mem, kv.shape[1]), lambda b, i: (b, 0)),
            wspec(wo),
            pl.BlockSpec((1, d), fixed),
        ],
        out_specs=(pl.BlockSpec((tm, d), row), pl.BlockSpec((tm, d), row)),
        compiler_params=_cparams(("parallel", "arbitrary")),
        name="mix_xattn",
    )(yd, ys, yg, w_out, h, g.reshape(1, d), wq, kv, wo, g_ffn.reshape(1, d))


def _ffn_up_kernel(x_ref, xp_ref, wg_ref, wu_ref, cw_ref, cb_ref, o_ref, xe_ref, ga_ref, ua_ref, gb_ref, ub_ref,
                   *, tm, tiles_per_seq, nf, d_ff):
    i = pl.program_id(0)
    j = pl.program_id(1)
    hl = CONV_HALO
    ck = FFN_CHUNK

    @pl.when(j == 0)
    def _():
        first = (i % tiles_per_seq) == 0
        prev = xp_ref[...]
        xe_ref[0:hl, :] = jnp.where(first, jnp.zeros_like(prev), prev)
        xe_ref[hl:hl + tm, :] = x_ref[...]
        gb_ref[...] = jnp.zeros_like(gb_ref)
        ub_ref[...] = jnp.zeros_like(ub_ref)

    xe = xe_ref[...]

    def matmuls(c0, g_out, u_out):
        g_out[...] = jnp.dot(xe, wg_ref[:, c0:c0 + ck], preferred_element_type=F32)
        u_out[...] = jnp.dot(xe, wu_ref[:, c0:c0 + ck], preferred_element_type=F32)

    def conv(h_ref, col):
        hh = h_ref[...]
        cw = cw_ref[:, pl.ds(col, ck)]
        return (hh[hl - 2:hl - 2 + tm] * cw[0:1] + hh[hl - 1:hl - 1 + tm] * cw[1:2]
                + hh[hl:hl + tm] * cw[2:3] + cb_ref[:, pl.ds(col, ck)])

    def epilogue(g_in, u_in, col):
        col = pl.multiple_of(col, ck)
        gate = conv(g_in, col)
        up = conv(u_in, pl.multiple_of(d_ff + col, ck))
        o_ref[:, pl.ds(col, ck)] = (gate / (1.0 + jnp.exp(-gate)) * up).astype(o_ref.dtype)

    col_a = j * (2 * ck)
    matmuls(0, ga_ref, ua_ref)
    epilogue(gb_ref, ub_ref, jnp.maximum(col_a - ck, 0))
    matmuls(ck, gb_ref, ub_ref)
    epilogue(ga_ref, ua_ref, col_a)

    @pl.when(j == nf - 1)
    def _():
        epilogue(gb_ref, ub_ref, col_a + ck)


def ffn_up(xn, w_up, conv_w, conv_b, layer, *, seq, tm):
    m, d = xn.shape
    d_ff = w_up.shape[2] // 2
    tf = 2 * FFN_CHUNK
    nf = d_ff // tf
    hl = CONV_HALO
    kern = functools.partial(_ffn_up_kernel, tm=tm, tiles_per_seq=seq // tm, nf=nf, d_ff=d_ff)
    raw = pltpu.VMEM((tm + hl, FFN_CHUNK), F32)
    return pl.pallas_call(
        kern,
        out_shape=jax.ShapeDtypeStruct((m, d_ff), BF16),
        grid=(m // tm, nf),
        in_specs=[
            pl.BlockSpec((tm, d), lambda i, j: (i, 0)),
            pl.BlockSpec((hl, d), lambda i, j: (jnp.maximum(i * (tm // hl) - 1, 0), 0)),
            pl.BlockSpec((None, d, tf), lambda i, j: (layer, 0, j)),
            pl.BlockSpec((None, d, tf), lambda i, j: (layer, 0, nf + j)),
            pl.BlockSpec((None, CONV_W, 2 * d_ff), lambda i, j: (layer, 0, 0)),
            pl.BlockSpec((None, 1, 2 * d_ff), lambda i, j: (layer, 0, 0)),
        ],
        out_specs=pl.BlockSpec((tm, d_ff), lambda i, j: (i, 0)),
        scratch_shapes=[pltpu.VMEM((tm + hl, d), BF16), raw, raw, raw, raw],
        compiler_params=_cparams(("parallel", "arbitrary")),
        name="ffn_up",
    )(xn, xn, w_up, w_up, conv_w, conv_b.reshape(conv_b.shape[0], 1, -1))


def _matmul_res_kernel(a_ref, w_ref, r_ref, o_ref):
    o_ref[...] = r_ref[...] + jnp.dot(a_ref[...], w_ref[...], preferred_element_type=F32)


def matmul_residual(a, w, layer, res, *, tm, tn):
    m, k = a.shape
    n = w.shape[2]
    return pl.pallas_call(
        _matmul_res_kernel,
        out_shape=jax.ShapeDtypeStruct((m, n), F32),
        grid=(m // tm, n // tn),
        in_specs=[
            pl.BlockSpec((tm, k), lambda i, j: (i, 0)),
            pl.BlockSpec((None, k, tn), lambda i, j: (layer, 0, j)),
            pl.BlockSpec((tm, tn), lambda i, j: (i, j)),
        ],
        out_specs=pl.BlockSpec((tm, tn), lambda i, j: (i, j)),
        compiler_params=_cparams(("parallel", "arbitrary")),
        name="matmul_residual",
    )(a, w, res)


def _rmsnorm_kernel(x_ref, g_ref, o_ref):
    x = x_ref[...]
    ms = jnp.mean(x * x, axis=-1, keepdims=True)
    o_ref[...] = x * lax.rsqrt(ms + EPS) * g_ref[...]


def rmsnorm(x, g, *, tm):
    m, d = x.shape
    return pl.pallas_call(
        _rmsnorm_kernel,
        out_shape=jax.ShapeDtypeStruct((m, d), F32),
        grid=(m // tm,),
        in_specs=[pl.BlockSpec((tm, d), lambda i: (i, 0)), pl.BlockSpec((1, d), lambda i: (0, 0))],
        out_specs=pl.BlockSpec((tm, d), lambda i: (i, 0)),
        compiler_params=_cparams(("parallel",)),
        name="final_rmsnorm",
    )(x, g.reshape(1, d))


def _prep_w_in(w):
    depth, d = w.shape[0], w.shape[1]
    scale = jnp.ones((C_OG,), F32)
    scale = scale.at[C_DQ:C_DK].set(DIFF_DQK ** -0.5 * LOG2E)
    scale = scale.at[C_SQ:C_SK].set(SWA_DH ** -0.5 * LOG2E)
    scale = scale.at[C_GQ:C_GK].set(GLA_DK ** -0.5)
    lr0 = C_OG
    og0 = lr0 + GLA_RANK
    main = w[:, :, :C_OG] * scale
    og = w[:, :, og0:og0 + GLA_HEADS * GLA_DV]
    lr = w[:, :, lr0:lr0 + GLA_RANK]
    pad = jnp.zeros((depth, d, N_PROJ - C_LR - GLA_RANK), w.dtype)
    return jnp.concatenate([main, og, lr, pad], axis=2).astype(BF16)


def kernel(x, mem, norm_mix_g, w_in, diff_lambda, diff_subln_g, swa_sinks, gla_gate_w2, gla_gate_b, gla_norm_g, w_out, norm_xa_g, norm_mem_g, xa_wq, xa_wkv, xa_wo, norm_ffn_g, ffn_w_up, ffn_conv_w, ffn_conv_b, ffn_w_down, final_norm_g):
    batch, seq, d = x.shape
    n_mem = mem.shape[1]
    depth = w_in.shape[0]
    t = batch * seq
    tm = min(512, seq)
    tm_big = min(1024, seq)

    h = x.reshape(t, d)
    memf = mem.reshape(batch * n_mem, d)

    w_in_b = _prep_w_in(w_in)
    w_out_b = w_out.astype(BF16)
    wq_b = (xa_wq * (XA_DH ** -0.5 * LOG2E)).astype(BF16)
    wkv_b = xa_wkv.astype(BF16)
    wo_b = xa_wo.astype(BF16)
    w_up_b = ffn_w_up.astype(BF16)
    w_down_b = ffn_w_down.astype(BF16)

    for l in range(depth):
        lambda_init = 0.8 - 0.6 * math.exp(-0.3 * l)
        linit = jnp.full((1, 1), lambda_init, F32)

        proj = norm_matmul(h, norm_mix_g[l], w_in_b, l, tm=tm_big, tn=1536)
        y_diff = diff_attention(proj, diff_lambda[l], linit, diff_subln_g[l], batch=batch, seq=seq)
        y_swa = swa_attention(proj, swa_sinks[l], batch=batch, seq=seq)
        w2p = jnp.zeros((LANES, GLA_HEADS * GLA_DK), F32).at[:GLA_RANK].set(gla_gate_w2[l]).astype(BF16)
        y_gla = gla_attention(proj, w2p, gla_gate_b[l], gla_norm_g[l], batch=batch, seq=seq)

        kv = norm_matmul(memf, norm_mem_g[l], wkv_b, l, tm=min(512, batch * n_mem), tn=512)
        h, xn = mix_xattn(y_diff, y_swa, y_gla, w_out_b, h, norm_xa_g[l], wq_b, kv, wo_b, l, norm_ffn_g[l],
                          batch=batch, seq=seq, n_mem=n_mem, tm=tm)

        act = ffn_up(xn, w_up_b, ffn_conv_w, ffn_conv_b, l, seq=seq, tm=tm_big)
        h = matmul_residual(act, w_down_b, l, h, tm=tm_big, tn=512)

    out = rmsnorm(h, final_norm_g, tm=tm)
    return out.reshape(batch, seq, d)
```

```python
import functools
import math

import jax
import jax.numpy as jnp
import numpy as np
from jax import lax
from jax.experimental import pallas as pl
from jax.experimental.pallas import tpu as pltpu

F32 = jnp.float32
BF16 = jnp.bfloat16
EPS = 1e-6
NEG = -1e30
LOG2E = math.log2(math.e)

LANES = 128
VMEM_LIMIT = 56 * 1024 * 1024

DIFF_HEADS = 8
DIFF_DQK = 32
DIFF_DV = 64
SWA_HEADS = 16
SWA_KV_HEADS = 2
SWA_DH = 64
SWA_WINDOW = 128
GLA_HEADS = 4
GLA_DK = 64
GLA_DV = 128
GLA_RANK = 16
GLA_TAU = 16.0
GLA_CHUNK = 64
XA_HEADS = 4
XA_DH = 128
CONV_W = 3
CONV_HALO = 16
FFN_CHUNK = 256

C_DQ, C_DK, C_DV = 0, 512, 1024
C_SQ, C_SK, C_SV = 1536, 2560, 2688
C_GQ, C_GK, C_GV = 2816, 3072, 3328
C_OG, C_LR = 3840, 4352
N_PROJ = 4608


def _cparams(sem):
    return pltpu.CompilerParams(dimension_semantics=sem, vmem_limit_bytes=VMEM_LIMIT)


def _norm_matmul_kernel(x_ref, g_ref, w_ref, o_ref, xn_ref):
    @pl.when(pl.program_id(1) == 0)
    def _():
        x = x_ref[...]
        ms = jnp.mean(x * x, axis=-1, keepdims=True)
        xn_ref[...] = (x * lax.rsqrt(ms + EPS) * g_ref[...]).astype(BF16)

    o_ref[...] = jnp.dot(xn_ref[...], w_ref[...], preferred_element_type=F32).astype(o_ref.dtype)


def norm_matmul(x, g, w, layer, *, tm, tn, out_dtype=BF16):
    m, k = x.shape
    n = w.shape[2]
    return pl.pallas_call(
        _norm_matmul_kernel,
        out_shape=jax.ShapeDtypeStruct((m, n), out_dtype),
        grid=(m // tm, n // tn),
        in_specs=[
            pl.BlockSpec((tm, k), lambda i, j: (i, 0)),
            pl.BlockSpec((1, k), lambda i, j: (0, 0)),
            pl.BlockSpec((None, k, tn), lambda i, j: (layer, 0, j)),
        ],
        out_specs=pl.BlockSpec((tm, tn), lambda i, j: (i, j)),
        scratch_shapes=[pltpu.VMEM((tm, k), BF16)],
        compiler_params=_cparams(("parallel", "arbitrary")),
        name="norm_matmul",
    )(x, g.reshape(1, k), w)


def _bf16_split3(x):
    parts = []
    r = np.float64(x)
    for _ in range(3):
        bits = np.array([r], np.float32).view(np.uint32)
        bits = (bits + np.uint32(0x7FFF) + ((bits >> np.uint32(16)) & np.uint32(1))) & np.uint32(0xFFFF0000)
        a = float(bits.view(np.float32)[0])
        parts.append(a)
        r = r - a
    return parts


LOG2E_PARTS = _bf16_split3(LOG2E)


def _diff_key_features(seq):
    j = np.arange(seq)
    f = np.zeros((seq, LANES), np.float32)
    for t in range(3):
        f[:, 2 * t] = j // LANES
        f[:, 2 * t + 1] = j % LANES
    f[:, 6] = 1.0
    f[:, 7] = 1.0
    return jnp.asarray(f).astype(BF16)


def _diff_diag_mask(tq, tk):
    assert tq == tk
    row = np.arange(4 * tq)[:, None] % tq
    col = np.arange(tk)[None, :]
    return jnp.asarray(np.where(row >= col, 0.0, NEG).astype(np.float32))


def _diff_kernel(lam_ref, linit_ref, g_ref, kf_ref, dm_ref, q_ref, k_ref, v_ref, o_ref,
                 s0_ref, s1_ref, s2_ref, s3_ref, m_ref, acc_ref, q4_ref, *, tq, tk, nk, nq):
    hp = pl.program_id(1)
    lane = lax.broadcasted_iota(jnp.int32, (tq, LANES), 1)
    left = lane < DIFF_DV
    ones = jnp.ones((tk, LANES), BF16)
    bufs = ((s0_ref, s1_ref), (s2_ref, s3_ref))

    lamv = lam_ref[...]
    lam1 = jnp.exp(jnp.sum(lamv[0:1] * lamv[1:2], axis=-1, keepdims=True))
    lam2 = jnp.exp(jnp.sum(lamv[2:3] * lamv[3:4], axis=-1, keepdims=True))
    linit = linit_ref[...]
    lam_full = lam1 - lam2 + linit

    def qk(kj, s_out):
        start = pl.multiple_of(jnp.minimum(kj, nk - 1) * tk, tk)
        ka = jnp.concatenate([k_ref[pl.ds(start, tk), :], kf_ref[pl.ds(start, tk), :]], axis=1)
        s_out[...] = lax.dot_general(q4_ref[...], ka, (((1,), (1,)), ((), ())), preferred_element_type=F32)

    def start_block(qi):
        pos = qi * tq + lax.broadcasted_iota(jnp.int32, (tq, LANES), 0)
        i_hi = (pos // LANES).astype(F32)
        i_lo = (pos % LANES).astype(F32)

        def features(hl):
            head = (2 * hp + hl + 1).astype(F32)
            slope = jnp.exp2(jnp.zeros((tq, LANES), F32) - head * (8.0 / DIFF_HEADS))
            f = jnp.zeros((tq, LANES), F32)
            for t, part in enumerate(LOG2E_PARTS):
                f = jnp.where(lane == 2 * t, slope * (part * LANES), f)
                f = jnp.where(lane == 2 * t + 1, slope * part, f)
            f = jnp.where(lane == 6, -slope * (LOG2E * LANES) * i_hi, f)
            f = jnp.where(lane == 7, -slope * LOG2E * i_lo, f)
            return f.astype(BF16)

        q = q_ref[pl.ds(pl.multiple_of(qi * tq, tq), tq), :]
        zero = jnp.zeros_like(q)
        feats = [features(0), features(1)]
        for c in range(4):
            q4_ref[c * tq:(c + 1) * tq, :] = jnp.concatenate(
                [jnp.where((lane // DIFF_DQK) == c, q, zero), feats[c // 2]], axis=1)
        m_ref[...] = jnp.full_like(m_ref, NEG)
        acc_ref[...] = jnp.zeros_like(acc_ref)
        qk(0, s0_ref)
        qk(1, s1_ref)

    def values(kj):
        start = pl.multiple_of(kj * tk, tk)
        return jnp.concatenate([v_ref[pl.ds(start, tk), :], ones], axis=1)

    def softmax_pv(qi, blocks):
        ss = [s_in[...] + dm_ref[...] if masked else s_in[...] for s_in, kj, masked in blocks]
        smax = ss[0] if len(ss) == 1 else jnp.maximum(ss[0], ss[1])
        m_old = m_ref[...]
        m_new = jnp.maximum(m_old, jnp.max(smax, axis=-1, keepdims=True))
        m_rep = jnp.concatenate([m_new] * (tk // LANES), axis=1)
        p = jnp.concatenate([jnp.exp2(s - m_rep).astype(BF16) for s in ss], axis=1)
        va = jnp.concatenate([values(kj) for _, kj, _ in blocks], axis=0)
        alpha = jnp.exp2(m_old - m_new)
        acc_ref[...] = (jnp.concatenate([alpha, alpha], axis=1) * acc_ref[...]
                        + jnp.dot(p, va, preferred_element_type=F32))
        m_ref[...] = m_new

    def finish_block(qi):
        acc = acc_ref[...]
        n = acc[:, 0:LANES] / acc[:, LANES:2 * LANES]
        a0 = n[0:tq] - lam_full * n[tq:2 * tq]
        a1 = n[2 * tq:3 * tq] - lam_full * n[3 * tq:4 * tq]
        o = jnp.where(left, a0, a1)
        sq = o * o
        ms0 = jnp.sum(jnp.where(left, sq, 0.0), axis=-1, keepdims=True) / DIFF_DV
        ms1 = jnp.sum(jnp.where(left, 0.0, sq), axis=-1, keepdims=True) / DIFF_DV
        ms = jnp.where(left, ms0, ms1)
        y = o * lax.rsqrt(ms + EPS) * g_ref[...] * (1.0 - linit)
        o_ref[pl.ds(pl.multiple_of(qi * tq, tq), tq), :] = y.astype(o_ref.dtype)

    def query_block(qi, carry):
        nfull = lax.div(qi * tq, tk)
        npair = lax.div(nfull, 2)

        def step(t, cur, nxt):
            qk(2 * t + 2, nxt[0])
            qk(2 * t + 3, nxt[1])
            softmax_pv(qi, [(cur[0], 2 * t, False), (cur[1], 2 * t + 1, False)])

        def pair(t, c):
            lax.cond(lax.rem(t, 2) == 0, lambda: step(t, bufs[0], bufs[1]), lambda: step(t, bufs[1], bufs[0]))
            return c

        lax.fori_loop(0, npair, pair, 0)

        def tail(cur):
            has_full = lax.rem(nfull, 2) == 1

            @pl.when(has_full)
            def _():
                softmax_pv(qi, [(cur[0], nfull - 1, False), (cur[1], nfull, True)])

            @pl.when(jnp.logical_not(has_full))
            def _():
                softmax_pv(qi, [(cur[0], nfull, True)])

        lax.cond(lax.rem(npair, 2) == 0, lambda: tail(bufs[0]), lambda: tail(bufs[1]))
        finish_block(qi)
        start_block(jnp.minimum(qi + 1, nq - 1))
        return carry

    start_block(0)
    lax.fori_loop(0, nq, query_block, 0)


def diff_attention(proj, lam, linit, subln_g, *, batch, seq, tq=256, tk=256):
    t = batch * seq
    g2 = jnp.concatenate([subln_g, subln_g]).reshape(1, LANES).astype(F32)
    kern = functools.partial(_diff_kernel, tq=tq, tk=tk, nk=seq // tk, nq=seq // tq)
    fixed = lambda b, h: (0, 0)
    sbuf = pltpu.VMEM((4 * tq, tk), F32)
    return pl.pallas_call(
        kern,
        out_shape=jax.ShapeDtypeStruct((t, DIFF_HEADS * DIFF_DV), BF16),
        grid=(batch, DIFF_HEADS // 2),
        in_specs=[
            pl.BlockSpec((4, DIFF_DQK), fixed),
            pl.BlockSpec((1, 1), fixed),
            pl.BlockSpec((1, LANES), fixed),
            pl.BlockSpec((seq, LANES), fixed),
            pl.BlockSpec((4 * tq, tk), fixed),
            pl.BlockSpec((seq, LANES), lambda b, h: (b, C_DQ // LANES + h)),
            pl.BlockSpec((seq, LANES), lambda b, h: (b, C_DK // LANES + h)),
            pl.BlockSpec((seq, LANES), lambda b, h: (b, C_DV // LANES + h)),
        ],
        out_specs=pl.BlockSpec((seq, LANES), lambda b, h: (b, h)),
        scratch_shapes=[sbuf, sbuf, sbuf, sbuf,
                        pltpu.VMEM((4 * tq, LANES), F32),
                        pltpu.VMEM((4 * tq, 2 * LANES), F32),
                        pltpu.VMEM((4 * tq, 2 * LANES), BF16)],
        compiler_params=_cparams(("parallel", "parallel")),
        name="diff_attention",
    )(lam.astype(F32), linit, g2, _diff_key_features(seq), _diff_diag_mask(tq, tk), proj, proj, proj)


def _swa_bias_tables():
    w = SWA_WINDOW
    rper = SWA_HEADS // SWA_KV_HEADS
    i = np.arange(w)[:, None]
    j = np.arange(2 * w)[None, :]
    dist = (i + w - j).astype(np.float64)
    valid = (dist >= 0) & (dist < w)
    slopes = 2.0 ** (-8.0 * np.arange(1, SWA_HEADS + 1, dtype=np.float64) / SWA_HEADS)
    out = np.empty((2, SWA_KV_HEADS, rper * w, 2 * w), np.float32)
    for variant, ok in enumerate((valid & (j >= w), valid)):
        for g in range(SWA_KV_HEADS):
            for r in range(rper):
                out[variant, g, r * w:(r + 1) * w] = np.where(ok, LOG2E * slopes[g * rper + r] * dist, 1e30)
    return jnp.asarray(out)


def _swa_kernel(b0_ref, b1_ref, sk_ref, q_ref, kp_ref, kc_ref, vp_ref, vc_ref, o_ref):
    g = pl.program_id(1)
    w = SWA_WINDOW
    rper = SWA_HEADS // SWA_KV_HEADS
    lane = lax.broadcasted_iota(jnp.int32, (3 * w, LANES), 1)
    mine = (lane // SWA_DH) == g

    def dup(prev_ref, cur_ref):
        x = jnp.concatenate([prev_ref[...], cur_ref[...]], axis=0).astype(F32)
        return jnp.where(mine, x, pltpu.roll(x, SWA_DH, axis=1)).astype(BF16)

    kk = dup(kp_ref, kc_ref)
    vv = jnp.concatenate([dup(vp_ref, vc_ref), jnp.ones((3 * w, LANES), BF16)], axis=1)
    qlane = lax.broadcasted_iota(jnp.int32, (w, LANES), 1)
    left = qlane < SWA_DH
    sink = sk_ref[0]

    for blk, b_ref in enumerate((b0_ref, b1_ref)):
        r0 = blk * w
        parts = []
        for p in range(rper // 2):
            q2 = q_ref[r0:r0 + w, p * LANES:(p + 1) * LANES]
            zero = jnp.zeros_like(q2)
            parts += [jnp.where(left, q2, zero), jnp.where(left, zero, q2)]
        qm = jnp.concatenate(parts, axis=0)
        s = lax.dot_general(qm, kk[r0:r0 + 2 * w], (((1,), (1,)), ((), ())), preferred_element_type=F32)
        u = s - b_ref[0, 0]
        m = jnp.maximum(jnp.max(u, axis=-1, keepdims=True), sink)
        pr = jnp.exp2(u - jnp.concatenate([m, m], axis=1))
        acc = jnp.dot(pr.astype(BF16), vv[r0:r0 + 2 * w], preferred_element_type=F32)
        l = acc[:, LANES:2 * LANES] + jnp.exp2(sink - m)
        o = acc[:, 0:LANES] / l
        for p in range(rper // 2):
            pair = jnp.where(left, o[(2 * p) * w:(2 * p + 1) * w], o[(2 * p + 1) * w:(2 * p + 2) * w])
            o_ref[r0:r0 + w, p * LANES:(p + 1) * LANES] = pair.astype(o_ref.dtype)


def swa_attention(proj, sinks, *, batch, seq):
    t = batch * seq
    w = SWA_WINDOW
    nb2 = seq // (2 * w)
    rper = SWA_HEADS // SWA_KV_HEADS
    bias = _swa_bias_tables()
    sink_rows = jnp.broadcast_to((sinks.astype(F32) * LOG2E).reshape(SWA_KV_HEADS, rper, 1, 1),
                                 (SWA_KV_HEADS, rper, w, LANES)).reshape(SWA_KV_HEADS, rper * w, LANES)
    qw = rper * SWA_DH
    kcol = C_SK // LANES
    vcol = C_SV // LANES

    def prev(b, n):
        return 2 * (b * nb2 + n) - jnp.where(n > 0, 1, 0)

    return pl.pallas_call(
        _swa_kernel,
        out_shape=jax.ShapeDtypeStruct((t, SWA_HEADS * SWA_DH), BF16),
        grid=(batch, SWA_KV_HEADS, nb2),
        in_specs=[
            pl.BlockSpec((1, 1, rper * w, 2 * w), lambda b, g, n: (jnp.minimum(n, 1), g, 0, 0)),
            pl.BlockSpec((1, 1, rper * w, 2 * w), lambda b, g, n: (1, g, 0, 0)),
            pl.BlockSpec((1, rper * w, LANES), lambda b, g, n: (g, 0, 0)),
            pl.BlockSpec((2 * w, qw), lambda b, g, n: (b * nb2 + n, C_SQ // qw + g)),
            pl.BlockSpec((w, LANES), lambda b, g, n: (prev(b, n), kcol)),
            pl.BlockSpec((2 * w, LANES), lambda b, g, n: (b * nb2 + n, kcol)),
            pl.BlockSpec((w, LANES), lambda b, g, n: (prev(b, n), vcol)),
            pl.BlockSpec((2 * w, LANES), lambda b, g, n: (b * nb2 + n, vcol)),
        ],
        out_specs=pl.BlockSpec((2 * w, qw), lambda b, g, n: (b * nb2 + n, g)),
        compiler_params=_cparams(("parallel", "parallel", "arbitrary")),
        name="swa_attention",
    )(bias, bias, sink_rows, proj, proj, proj, proj, proj)


def _gla_masks(tb):
    c = GLA_CHUNK
    t = np.arange(tb)[:, None]
    s = np.arange(tb)[None, :]
    same = (t // c) == (s // c)
    tri = same & (s <= t)
    return jnp.asarray(np.concatenate([tri, same], axis=0).astype(np.float32)).astype(BF16)


def _gla_kernel(mask_ref, w2_ref, gb_ref, ng_ref, q_ref, k_ref, v_ref, og_ref, lr_ref, o_ref, st_ref, *, tb):
    c = GLA_CHUNK
    nchunk = tb // c

    @pl.when(pl.program_id(2) == 0)
    def _():
        st_ref[...] = jnp.zeros_like(st_ref)

    z = jnp.dot(lr_ref[...], w2_ref[...], preferred_element_type=F32) + gb_ref[...]
    log_a = (jnp.minimum(z, 0.0) - jnp.log1p(jnp.exp(-jnp.abs(z)))) / GLA_TAU

    hi = log_a.astype(BF16)
    lo = (log_a - hi.astype(F32)).astype(BF16)
    hl = jnp.concatenate([hi, lo], axis=1)
    cs = jnp.dot(mask_ref[...], hl, preferred_element_type=F32)
    b = cs[0:tb, 0:LANES] + cs[0:tb, LANES:2 * LANES]
    b_last = cs[tb:2 * tb, 0:LANES] + cs[tb:2 * tb, LANES:2 * LANES]

    qf = q_ref[...].astype(F32)
    kf = k_ref[...].astype(F32)
    q_dec = (qf * jnp.exp(b)).astype(BF16)
    k_inv = (kf * jnp.exp(-b)).astype(BF16)
    k_end = (kf * jnp.exp(b_last - b)).astype(BF16)
    decay = jnp.exp(b_last)

    lane = lax.broadcasted_iota(jnp.int32, (tb, LANES), 1)
    left = lane < GLA_DK
    zero = jnp.zeros_like(q_dec)
    qd = [jnp.where(left, q_dec, zero), jnp.where(left, zero, q_dec)]
    tri = mask_ref[0:tb, :] > 0
    v = v_ref[...]

    intra = []
    for h in range(2):
        a = lax.dot_general(qd[h], k_inv, (((1,), (1,)), ((), ())), preferred_element_type=F32)
        a = jnp.where(tri, a, 0.0).astype(BF16)
        intra.append(jnp.dot(a, v[:, h * GLA_DV:(h + 1) * GLA_DV], preferred_element_type=F32))

    srow = lax.broadcasted_iota(jnp.int32, (2 * GLA_DV, LANES), 0) // GLA_DV
    scol = lax.broadcasted_iota(jnp.int32, (2 * GLA_DV, LANES), 1) // GLA_DK
    own = srow == scol
    state = st_ref[...]
    inter = []
    for n in range(nchunk):
        r0, r1 = n * c, (n + 1) * c
        inter.append(lax.dot_general(q_dec[r0:r1], state.astype(BF16), (((1,), (1,)), ((), ())),
                                     preferred_element_type=F32))
        kv_t = lax.dot_general(v[r0:r1], k_end[r0:r1], (((0,), (0,)), ((), ())),
                               preferred_element_type=F32)
        state = state * decay[r0:r0 + 1] + jnp.where(own, kv_t, 0.0)
    st_ref[...] = state
    o_inter = jnp.concatenate(inter, axis=0)

    og = og_ref[...].astype(F32)
    gate = og / (1.0 + jnp.exp(-og))
    for h in range(2):
        o = intra[h] + o_inter[:, h * GLA_DV:(h + 1) * GLA_DV]
        ms = jnp.mean(o * o, axis=-1, keepdims=True)
        y = o * lax.rsqrt(ms + EPS) * ng_ref[...] * gate[:, h * GLA_DV:(h + 1) * GLA_DV]
        o_ref[:, h * GLA_DV:(h + 1) * GLA_DV] = y.astype(o_ref.dtype)


def gla_attention(proj, w2p, gate_b, norm_g, *, batch, seq, tb=512):
    tb = min(tb, seq)
    t = batch * seq
    nblk = seq // tb
    masks = _gla_masks(tb)
    kern = functools.partial(_gla_kernel, tb=tb)
    w256 = 2 * GLA_DV
    return pl.pallas_call(
        kern,
        out_shape=jax.ShapeDtypeStruct((t, GLA_HEADS * GLA_DV), BF16),
        grid=(batch, GLA_HEADS // 2, nblk),
        in_specs=[
            pl.BlockSpec((2 * tb, tb), lambda b, h, n: (0, 0)),
            pl.BlockSpec((LANES, LANES), lambda b, h, n: (0, h)),
            pl.BlockSpec((1, LANES), lambda b, h, n: (0, h)),
            pl.BlockSpec((1, GLA_DV), lambda b, h, n: (0, 0)),
            pl.BlockSpec((tb, LANES), lambda b, h, n: (b * nblk + n, C_GQ // LANES + h)),
            pl.BlockSpec((tb, LANES), lambda b, h, n: (b * nblk + n, C_GK // LANES + h)),
            pl.BlockSpec((tb, w256), lambda b, h, n: (b * nblk + n, C_GV // w256 + h)),
            pl.BlockSpec((tb, w256), lambda b, h, n: (b * nblk + n, C_OG // w256 + h)),
            pl.BlockSpec((tb, LANES), lambda b, h, n: (b * nblk + n, C_LR // LANES)),
        ],
        out_specs=pl.BlockSpec((tb, w256), lambda b, h, n: (b * nblk + n, h)),
        scratch_shapes=[pltpu.VMEM((2 * GLA_DV, LANES), F32)],
        compiler_params=_cparams(("parallel", "parallel", "arbitrary")),
        name="gla_attention",
    )(masks, w2p, gate_b.reshape(1, -1).astype(F32), norm_g.reshape(1, -1).astype(F32),
      proj, proj, proj, proj, proj)


def _mix_xattn_kernel(yd_ref, ys_ref, yg_ref, w_ref, h_ref, g_ref, wq_ref, kv_ref, wo_ref, gf_ref, o_ref, xn_ref):
    kd, ks = yd_ref.shape[1], ys_ref.shape[1]
    acc = jnp.dot(yd_ref[...], w_ref[0:kd, :], preferred_element_type=F32)
    acc += jnp.dot(ys_ref[...], w_ref[kd:kd + ks, :], preferred_element_type=F32)
    acc += jnp.dot(yg_ref[...], w_ref[kd + ks:, :], preferred_element_type=F32)
    x = h_ref[...] + acc
    ms = jnp.mean(x * x, axis=-1, keepdims=True)
    xn = (x * lax.rsqrt(ms + EPS) * g_ref[...]).astype(BF16)
    q = jnp.dot(xn, wq_ref[...], preferred_element_type=F32).astype(BF16)
    d_xa = XA_HEADS * XA_DH
    outs = []
    for hd in range(XA_HEADS):
        kh = kv_ref[:, hd * XA_DH:(hd + 1) * XA_DH]
        vh = kv_ref[:, d_xa + hd * XA_DH:d_xa + (hd + 1) * XA_DH]
        s = lax.dot_general(q[:, hd * XA_DH:(hd + 1) * XA_DH], kh, (((1,), (1,)), ((), ())),
                            preferred_element_type=F32)
        m = jnp.max(s, axis=-1, keepdims=True)
        p = jnp.exp2(s - m)
        l = jnp.sum(p, axis=-1, keepdims=True)
        outs.append((jnp.dot(p.astype(BF16), vh, preferred_element_type=F32) / l).astype(BF16))
    o = jnp.concatenate(outs, axis=1)
    y = x + jnp.dot(o, wo_ref[...], preferred_element_type=F32)
    o_ref[...] = y
    ms2 = jnp.mean(y * y, axis=-1, keepdims=True)
    xn_ref[...] = (y * lax.rsqrt(ms2 + EPS) * gf_ref[...]).astype(BF16)


def mix_xattn(yd, ys, yg, w_out, h, g, wq, kv, wo, layer, g_ffn, *, batch, seq, n_mem, tm):
    m, d = h.shape
    nt = seq // tm
    fixed = lambda b, i: (0, 0)
    row = lambda b, i: (b * nt + i, 0)
    once = pl.Buffered(1)
    wspec = lambda w: pl.BlockSpec((None,) + w.shape[1:], lambda b, i: (layer, 0, 0), pipeline_mode=once)
    return pl.pallas_call(
        _mix_xattn_kernel,
        out_shape=(jax.ShapeDtypeStruct((m, d), F32), jax.ShapeDtypeStruct((m, d), BF16)),
        grid=(batch, nt),
        in_specs=[
            pl.BlockSpec((tm, yd.shape[1]), row),
            pl.BlockSpec((tm, ys.shape[1]), row),
            pl.BlockSpec((tm, yg.shape[1]), row),
            wspec(w_out),
            pl.BlockSpec((tm, d), row),
            pl.BlockSpec((1, d), fixed),
            wspec(wq),
            pl.BlockSpec((n_mem, kv.shape[1]), lambda b, i: (b, 0)),
            wspec(wo),
            pl.BlockSpec((1, d), fixed),
        ],
        out_specs=(pl.BlockSpec((tm, d), row), pl.BlockSpec((tm, d), row)),
        compiler_params=_cparams(("parallel", "arbitrary")),
        name="mix_xattn",
    )(yd, ys, yg, w_out, h, g.reshape(1, d), wq, kv, wo, g_ffn.reshape(1, d))


def _ffn_up_kernel(x_ref, xp_ref, wg_ref, wu_ref, cw_ref, cb_ref, o_ref, xe_ref, ga_ref, ua_ref, gb_ref, ub_ref,
                   *, tm, tiles_per_seq, nf, d_ff):
    i = pl.program_id(0)
    j = pl.program_id(1)
    hl = CONV_HALO
    ck = FFN_CHUNK

    @pl.when(j == 0)
    def _():
        first = (i % tiles_per_seq) == 0
        prev = xp_ref[...]
        xe_ref[0:hl, :] = jnp.where(first, jnp.zeros_like(prev), prev)
        xe_ref[hl:hl + tm, :] = x_ref[...]
        gb_ref[...] = jnp.zeros_like(gb_ref)
        ub_ref[...] = jnp.zeros_like(ub_ref)

    xe = xe_ref[...]

    def matmuls(c0, g_out, u_out):
        g_out[...] = jnp.dot(xe, wg_ref[:, c0:c0 + ck], preferred_element_type=F32)
        u_out[...] = jnp.dot(xe, wu_ref[:, c0:c0 + ck], preferred_element_type=F32)

    def conv(h_ref, col):
        hh = h_ref[...]
        cw = cw_ref[:, pl.ds(col, ck)]
        return (hh[hl - 2:hl - 2 + tm] * cw[0:1] + hh[hl - 1:hl - 1 + tm] * cw[1:2]
                + hh[hl:hl + tm] * cw[2:3] + cb_ref[:, pl.ds(col, ck)])

    def epilogue(g_in, u_in, col):
        col = pl.multiple_of(col, ck)
        gate = conv(g_in, col)
        up = conv(u_in, pl.multiple_of(d_ff + col, ck))
        o_ref[:, pl.ds(col, ck)] = (gate / (1.0 + jnp.exp(-gate)) * up).astype(o_ref.dtype)

    col_a = j * (2 * ck)
    matmuls(0, ga_ref, ua_ref)
    epilogue(gb_ref, ub_ref, jnp.maximum(col_a - ck, 0))
    matmuls(ck, gb_ref, ub_ref)
    epilogue(ga_ref, ua_ref, col_a)

    @pl.when(j == nf - 1)
    def _():
        epilogue(gb_ref, ub_ref, col_a + ck)


def ffn_up(xn, w_up, conv_w, conv_b, layer, *, seq, tm):
    m, d = xn.shape
    d_ff = w_up.shape[2] // 2
    tf = 2 * FFN_CHUNK
    nf = d_ff // tf
    hl = CONV_HALO
    kern = functools.partial(_ffn_up_kernel, tm=tm, tiles_per_seq=seq // tm, nf=nf, d_ff=d_ff)
    raw = pltpu.VMEM((tm + hl, FFN_CHUNK), F32)
    return pl.pallas_call(
        kern,
        out_shape=jax.ShapeDtypeStruct((m, d_ff), BF16),
        grid=(m // tm, nf),
        in_specs=[
            pl.BlockSpec((tm, d), lambda i, j: (i, 0)),
            pl.BlockSpec((hl, d), lambda i, j: (jnp.maximum(i * (tm // hl) - 1, 0), 0)),
            pl.BlockSpec((None, d, tf), lambda i, j: (layer, 0, j)),
            pl.BlockSpec((None, d, tf), lambda i, j: (layer, 0, nf + j)),
            pl.BlockSpec((None, CONV_W, 2 * d_ff), lambda i, j: (layer, 0, 0)),
            pl.BlockSpec((None, 1, 2 * d_ff), lambda i, j: (layer, 0, 0)),
        ],
        out_specs=pl.BlockSpec((tm, d_ff), lambda i, j: (i, 0)),
        scratch_shapes=[pltpu.VMEM((tm + hl, d), BF16), raw, raw, raw, raw],
        compiler_params=_cparams(("parallel", "arbitrary")),
        name="ffn_up",
    )(xn, xn, w_up, w_up, conv_w, conv_b.reshape(conv_b.shape[0], 1, -1))


def _matmul_res_kernel(a_ref, w_ref, r_ref, o_ref):
    o_ref[...] = r_ref[...] + jnp.dot(a_ref[...], w_ref[...], preferred_element_type=F32)


def matmul_residual(a, w, layer, res, *, tm, tn):
    m, k = a.shape
    n = w.shape[2]
    return pl.pallas_call(
        _matmul_res_kernel,
        out_shape=jax.ShapeDtypeStruct((m, n), F32),
        grid=(m // tm, n // tn),
        in_specs=[
            pl.BlockSpec((tm, k), lambda i, j: (i, 0)),
            pl.BlockSpec((None, k, tn), lambda i, j: (layer, 0, j)),
            pl.BlockSpec((tm, tn), lambda i, j: (i, j)),
        ],
        out_specs=pl.BlockSpec((tm, tn), lambda i, j: (i, j)),
        compiler_params=_cparams(("parallel", "arbitrary")),
        name="matmul_residual",
    )(a, w, res)


def _rmsnorm_kernel(x_ref, g_ref, o_ref):
    x = x_ref[...]
    ms = jnp.mean(x * x, axis=-1, keepdims=True)
    o_ref[...] = x * lax.rsqrt(ms + EPS) * g_ref[...]


def rmsnorm(x, g, *, tm):
    m, d = x.shape
    return pl.pallas_call(
        _rmsnorm_kernel,
        out_shape=jax.ShapeDtypeStruct((m, d), F32),
        grid=(m // tm,),
        in_specs=[pl.BlockSpec((tm, d), lambda i: (i, 0)), pl.BlockSpec((1, d), lambda i: (0, 0))],
        out_specs=pl.BlockSpec((tm, d), lambda i: (i, 0)),
        compiler_params=_cparams(("parallel",)),
        name="final_rmsnorm",
    )(x, g.reshape(1, d))


def _prep_w_in(w):
    depth, d = w.shape[0], w.shape[1]
    scale = jnp.ones((C_OG,), F32)
    scale = scale.at[C_DQ:C_DK].set(DIFF_DQK ** -0.5 * LOG2E)
    scale = scale.at[C_SQ:C_SK].set(SWA_DH ** -0.5 * LOG2E)
    scale = scale.at[C_GQ:C_GK].set(GLA_DK ** -0.5)
    lr0 = C_OG
    og0 = lr0 + GLA_RANK
    main = w[:, :, :C_OG] * scale
    og = w[:, :, og0:og0 + GLA_HEADS * GLA_DV]
    lr = w[:, :, lr0:lr0 + GLA_RANK]
    pad = jnp.zeros((depth, d, N_PROJ - C_LR - GLA_RANK), w.dtype)
    return jnp.concatenate([main, og, lr, pad], axis=2).astype(BF16)


def kernel(x, mem, norm_mix_g, w_in, diff_lambda, diff_subln_g, swa_sinks, gla_gate_w2, gla_gate_b, gla_norm_g, w_out, norm_xa_g, norm_mem_g, xa_wq, xa_wkv, xa_wo, norm_ffn_g, ffn_w_up, ffn_conv_w, ffn_conv_b, ffn_w_down, final_norm_g):
    batch, seq, d = x.shape
    n_mem = mem.shape[1]
    depth = w_in.shape[0]
    t = batch * seq
    tm = min(512, seq)
    tm_big = min(1024, seq)

    h = x.reshape(t, d)
    memf = mem.reshape(batch * n_mem, d)

    w_in_b = _prep_w_in(w_in)
    w_out_b = w_out.astype(BF16)
    wq_b = (xa_wq * (XA_DH ** -0.5 * LOG2E)).astype(BF16)
    wkv_b = xa_wkv.astype(BF16)
    wo_b = xa_wo.astype(BF16)
    w_up_b = ffn_w_up.astype(BF16)
    w_down_b = ffn_w_down.astype(BF16)

    for l in range(depth):
        lambda_init = 0.8 - 0.6 * math.exp(-0.3 * l)
        linit = jnp.full((1, 1), lambda_init, F32)

        proj = norm_matmul(h, norm_mix_g[l], w_in_b, l, tm=tm_big, tn=1536)
        y_diff = diff_attention(proj, diff_lambda[l], linit, diff_subln_g[l], batch=batch, seq=seq)
        y_swa = swa_attention(proj, swa_sinks[l], batch=batch, seq=seq)
        w2p = jnp.zeros((LANES, GLA_HEADS * GLA_DK), F32).at[:GLA_RANK].set(gla_gate_w2[l]).astype(BF16)
        y_gla = gla_attention(proj, w2p, gla_gate_b[l], gla_norm_g[l], batch=batch, seq=seq)

        kv = norm_matmul(memf, norm_mem_g[l], wkv_b, l, tm=min(512, batch * n_mem), tn=512)
        h, xn = mix_xattn(y_diff, y_swa, y_gla, w_out_b, h, norm_xa_g[l], wq_b, kv, wo_b, l, norm_ffn_g[l],
                          batch=batch, seq=seq, n_mem=n_mem, tm=tm)

        act = ffn_up(xn, w_up_b, ffn_conv_w, ffn_conv_b, l, seq=seq, tm=tm_big)
        h = matmul_residual(act, w_down_b, l, h, tm=tm_big, tn=512)

    out = rmsnorm(h, final_norm_g, tm=tm)
    return out.reshape(batch, seq, d)
```
